```python
import math
import jax
import jax.numpy as jnp
from jax import lax
import numpy as np

D_MODEL = 1024
BATCH = 8
SEQ = 2048
DEPTH = 4

CTX_LEN = 256
GRID_W = 64
HEAD_DIM = 64
D_MIX = D_MODEL
GROUP_W = D_MIX // 4
N_GROUP_HEADS = GROUP_W // HEAD_DIM
N_DIR = 2
NORM_EPS = 1e-6
N_MOD = 6
GDN_HEADS = N_GROUP_HEADS
GDN_CONV = 5
GDN_CHUNK = 64
SWA_Q_HEADS = N_GROUP_HEADS
SWA_KV_HEADS = 2
SWA_WINDOW = 128
SWA_BLOCK = 128
ROPE_THETA = 10000.0
GMLP_GROUPS = N_GROUP_HEADS
GMLP_CHUNK = 128
MLSTM_HEADS = N_GROUP_HEADS
MLSTM_CHUNK = 64
D_FF = 4 * D_MODEL

IN_SPLITS = (GROUP_W, GROUP_W, GROUP_W, GROUP_W, N_DIR * GDN_HEADS, N_DIR * GDN_HEADS,
             GROUP_W, SWA_KV_HEADS * HEAD_DIM, SWA_KV_HEADS * HEAD_DIM,
             GROUP_W, GROUP_W,
             GROUP_W, GROUP_W, GROUP_W, GROUP_W, N_DIR * MLSTM_HEADS, N_DIR * MLSTM_HEADS)
D_IN = sum(IN_SPLITS)

kernel_name = 'hybrid_parallel_groups_flow_block'


def rmsnorm(x, w):
    xf = x.astype(jnp.float32)
    y = xf * lax.rsqrt(jnp.mean(xf * xf, axis=-1, keepdims=True) + NORM_EPS)
    return (y * w.astype(jnp.float32)).astype(x.dtype)


def l2norm(x):
    xf = x.astype(jnp.float32)
    return xf * lax.rsqrt(jnp.sum(xf * xf, axis=-1, keepdims=True) + NORM_EPS)


def heads(t, n):
    return t.reshape(t.shape[:-1] + (n, t.shape[-1] // n))


def split_cols(z):
    out, start = [], 0
    for size in IN_SPLITS:
        out.append(z[..., start:start + size])
        start += size
    return out


def modulate(h, shift, scale):
    return h * (1 + scale) + shift


def dwconv_centred(x, w):
    k, ch = w.shape
    return lax.conv_general_dilated(x, w[:, None, :].astype(x.dtype), window_strides=(1,),
                                    padding=[(k // 2, k // 2)], dimension_numbers=('NWC', 'WIO', 'NWC'),
                                    feature_group_count=ch)


def to_chunks(t, chunk):
    bsz, length, h = t.shape[:3]
    t = t.astype(jnp.float32).reshape((bsz, length // chunk, chunk, h) + t.shape[3:])
    return jnp.moveaxis(t, (1, 3), (0, 2))


def from_chunks(t):
    t = jnp.moveaxis(t, (0, 2), (1, 3))
    return t.reshape((t.shape[0], t.shape[1] * t.shape[2]) + t.shape[3:])


def gdn_scan(q, k, v, g, beta, state):
    dk, dv = q.shape[-1], v.shape[-1]
    q = to_chunks(q, GDN_CHUNK) * dk ** -0.5
    k = to_chunks(k, GDN_CHUNK)
    v = to_chunks(v, GDN_CHUNK)
    g = to_chunks(g, GDN_CHUNK)
    beta = to_chunks(beta, GDN_CHUNK)
    idx = jnp.arange(GDN_CHUNK)
    incl = idx[:, None] >= idx[None, :]
    strict = idx[:, None] > idx[None, :]
    gcum = jnp.cumsum(g, axis=-1)
    decay = jnp.exp(jnp.where(incl, gcum[..., :, None] - gcum[..., None, :], -jnp.inf))
    kbeta = k * beta[..., None]
    a = jnp.where(strict, jnp.einsum('nbhik,nbhjk->nbhij', kbeta, k) * decay, 0.0)
    rhs = jnp.concatenate([v * beta[..., None], kbeta * jnp.exp(gcum)[..., None]], axis=-1)
    sol = lax.linalg.triangular_solve(a + jnp.eye(GDN_CHUNK, dtype=jnp.float32), rhs,
                                      left_side=True, lower=True)
    u, w = sol[..., :dv], sol[..., dv:]
    attn = jnp.einsum('nbhik,nbhjk->nbhij', q, k) * decay
    q_dec = q * jnp.exp(gcum)[..., None]
    k_dec = k * jnp.exp(gcum[..., -1:] - gcum)[..., None]
    g_last = jnp.exp(gcum[..., -1])

    def step(s, inp):
        u_c, w_c, attn_c, q_c, k_c, gl_c = inp
        v_new = u_c - jnp.einsum('bhck,bhkv->bhcv', w_c, s)
        o = jnp.einsum('bhck,bhkv->bhcv', q_c, s) + jnp.einsum('bhij,bhjv->bhiv', attn_c, v_new)
        s = s * gl_c[..., None, None] + jnp.einsum('bhck,bhcv->bhkv', k_c, v_new)
        return s, o

    state, o = lax.scan(step, state, (u, w, attn, q_dec, k_dec, g_last))
    return from_chunks(o), state


def mlstm_scan(q, k, v, ig, fg, state):
    dk = q.shape[-1]
    q = to_chunks(q, MLSTM_CHUNK) * dk ** -0.5
    k = to_chunks(k, MLSTM_CHUNK)
    v = to_chunks(v, MLSTM_CHUNK)
    ig = to_chunks(ig, MLSTM_CHUNK)
    logf = jax.nn.log_sigmoid(to_chunks(fg, MLSTM_CHUNK))
    idx = jnp.arange(MLSTM_CHUNK)
    incl = idx[:, None] >= idx[None, :]
    b = jnp.cumsum(logf, axis=-1)
    dmat = jnp.where(incl, b[..., :, None] - b[..., None, :] + ig[..., None, :], -jnp.inf)
    dmax = jnp.max(dmat, axis=-1)
    qk = jnp.einsum('nbhik,nbhjk->nbhij', q, k)
    b_last = b[..., -1]
    w_state = b[..., -1:] - b + ig
    w_state_max = jnp.max(w_state, axis=-1)

    def step(carry, inp):
        c_mem, n_vec, m = carry
        q_c, k_c, v_c, b_c, d_c, dmax_c, qk_c, ws_c, wsmax_c, bl_c = inp
        inter = b_c + m[..., None]
        m_t = jnp.maximum(inter, dmax_c)
        s = qk_c * jnp.exp(d_c - m_t[..., None])
        w_inter = jnp.exp(inter - m_t)
        num = (w_inter[..., None] * jnp.einsum('bhck,bhkv->bhcv', q_c, c_mem)
               + jnp.einsum('bhij,bhjv->bhiv', s, v_c))
        den = w_inter * jnp.einsum('bhck,bhk->bhc', q_c, n_vec) + jnp.sum(s, axis=-1)
        h = num / jnp.maximum(jnp.abs(den), jnp.exp(-m_t))[..., None]
        m_new = jnp.maximum(bl_c + m, wsmax_c)
        carry_decay = jnp.exp(bl_c + m - m_new)
        k_w = k_c * jnp.exp(ws_c - m_new[..., None])[..., None]
        c_mem = carry_decay[..., None, None] * c_mem + jnp.einsum('bhck,bhcv->bhkv', k_w, v_c)
        n_vec = carry_decay[..., None] * n_vec + jnp.sum(k_w, axis=-2)
        return (c_mem, n_vec, m_new), h

    state, h = lax.scan(step, state, (q, k, v, b, dmat, dmax, qk, w_state, w_state_max, b_last))
    return from_chunks(h), state


def bidirectional_prefixed(scan_fn, ctx_seq, ctx_gates, lat_seq, lat_gates, state0):
    out_x, out_c = 0.0, 0.0
    for d in range(N_DIR):
        flip = (lambda t: jnp.flip(t, axis=1)) if d == 1 else (lambda t: t)
        c_args = [flip(t) for t in ctx_seq] + [flip(gt[:, :, d]) for gt in ctx_gates]
        x_args = [flip(t) for t in lat_seq] + [flip(gt[:, :, d]) for gt in lat_gates]
        o_c, ctx_state = scan_fn(*c_args, state0)
        o_x, _ = scan_fn(*x_args, ctx_state)
        out_c = out_c + flip(o_c)
        out_x = out_x + flip(o_x)
    return out_x, out_c


def gdn_mixer(lat, cx, conv_w, a_log, dt_bias, norm_w):
    dtype = lat[0].dtype

    def branch(q, k, v, a, b):
        qkv = jax.nn.silu(dwconv_centred(jnp.concatenate([q, k, v], axis=-1), conv_w))
        q, k, v = jnp.split(qkv, 3, axis=-1)
        g = -jnp.exp(a_log.astype(jnp.float32)) * jax.nn.softplus(
            heads(a.astype(jnp.float32), N_DIR) + dt_bias.astype(jnp.float32))
        beta = jax.nn.sigmoid(heads(b.astype(jnp.float32), N_DIR))
        seq = (l2norm(heads(q, GDN_HEADS)), l2norm(heads(k, GDN_HEADS)), heads(v, GDN_HEADS))
        return seq, (g, beta)

    (sx, gx), (sc, gcx) = branch(*lat[:3], *lat[4:]), branch(*cx[:3], *cx[4:])
    state0 = jnp.zeros((lat[0].shape[0], GDN_HEADS, HEAD_DIM, HEAD_DIM), jnp.float32)
    ox, oc = bidirectional_prefixed(gdn_scan, sc, gcx, sx, gx, state0)

    def finish(o, z):
        o = rmsnorm(o, norm_w) * jax.nn.silu(heads(z.astype(jnp.float32), GDN_HEADS))
        return o.reshape(o.shape[:2] + (GROUP_W,)).astype(dtype)

    return finish(ox, lat[3]), finish(oc, cx[3])


def axial_rope(length, dtype):
    rows = length // GRID_W
    row = jnp.repeat(jnp.arange(rows), GRID_W).astype(jnp.float32)
    col = (jnp.arange(rows * GRID_W) % GRID_W).astype(jnp.float32)
    n_freq = HEAD_DIM // 4
    inv = jnp.power(ROPE_THETA, -jnp.arange(n_freq, dtype=jnp.float32) / n_freq)
    ang = jnp.concatenate([row[:, None] * inv, col[:, None] * inv], axis=-1)
    return jnp.cos(ang)[:, None, :].astype(dtype), jnp.sin(ang)[:, None, :].astype(dtype)


def apply_rope(t, cos, sin):
    half = t.shape[-1] // 2
    t1, t2 = t[..., :half], t[..., half:]
    return jnp.concatenate([t1 * cos - t2 * sin, t1 * sin + t2 * cos], axis=-1)


def sink_softmax(sink_logit, scores):
    sk = jnp.broadcast_to(sink_logit, scores.shape[:-1] + (1,))
    return jax.nn.softmax(jnp.concatenate([sk, scores], axis=-1), axis=-1)[..., 1:]


def banded_window_attention(q, k, v, kc, vc, sink):
    bsz, length, hq, dh = q.shape
    hkv = k.shape[2]
    grp = hq // hkv
    blk = SWA_BLOCK
    nb = length // blk
    scale = dh ** -0.5
    qb = q.reshape(bsz, nb, blk, hkv, grp, dh)

    def band(t):
        tp = jnp.pad(t, ((0, 0), (blk, blk), (0, 0), (0, 0))).reshape(bsz, nb + 2, blk, hkv, dh)
        return jnp.concatenate([tp[:, :-2], tp[:, 1:-1], tp[:, 2:]], axis=2)

    kb, vb = band(k), band(v)
    s_loc = jnp.einsum('bnqhgd,bnkhd->bnhgqk', qb, kb).astype(jnp.float32) * scale
    s_ctx = jnp.einsum('bnqhgd,bchd->bnhgqc', qb, kc).astype(jnp.float32) * scale
    qi = jnp.arange(blk)[:, None]
    kj = jnp.arange(3 * blk)[None, :]
    kpos = jnp.arange(nb)[:, None, None] * blk + kj[None] - blk
    mask = (jnp.abs(kj - blk - qi) <= SWA_WINDOW)[None] & (kpos >= 0) & (kpos < length)
    s_loc = jnp.where(mask[None, :, None, None], s_loc, -jnp.inf)
    sk = sink.astype(jnp.float32).reshape(hkv, grp)[None, None, :, :, None, None]
    p = sink_softmax(sk, jnp.concatenate([s_loc, s_ctx], axis=-1)).astype(v.dtype)
    p_loc, p_ctx = p[..., :3 * blk], p[..., 3 * blk:]
    o = (jnp.einsum('bnhgqk,bnkhd->bnqhgd', p_loc, vb)
         + jnp.einsum('bnhgqc,bchd->bnqhgd', p_ctx, vc))
    return o.reshape(bsz, length, hq * dh)


def context_attention(qc, kc, vc, sink):
    bsz, lc, hq, dh = qc.shape
    hkv = kc.shape[2]
    grp = hq // hkv
    q = qc.reshape(bsz, lc, hkv, grp, dh)
    s = jnp.einsum('bqhgd,bkhd->bhgqk', q, kc).astype(jnp.float32) * dh ** -0.5
    sk = sink.astype(jnp.float32).reshape(hkv, grp)[None, :, :, None, None]
    p = sink_softmax(sk, s).astype(vc.dtype)
    return jnp.einsum('bhgqk,bkhd->bqhgd', p, vc).reshape(bsz, lc, hq * dh)


def swa_mixer(lat, cx, sink):
    qx, kx, vx = lat
    qc, kc, vc = cx
    cos, sin = axial_rope(qx.shape[1], qx.dtype)
    qx = apply_rope(heads(qx, SWA_Q_HEADS), cos, sin)
    kx = apply_rope(heads(kx, SWA_KV_HEADS), cos, sin)
    kc, vc = heads(kc, SWA_KV_HEADS), heads(vc, SWA_KV_HEADS)
    o_x = banded_window_attention(qx, kx, heads(vx, SWA_KV_HEADS), kc, vc, sink)
    o_c = context_attention(heads(qc, SWA_Q_HEADS), kc, vc, sink)
    return o_x, o_c


def gmlp_mixer(u, v, w_s, b_s, norm_w):
    bsz, length, _ = u.shape
    n = length // GMLP_CHUNK
    u = jax.nn.gelu(u)
    v = rmsnorm(jax.nn.gelu(v), norm_w)
    vb = v.reshape(bsz, n, GMLP_CHUNK, GMLP_GROUPS, GROUP_W // GMLP_GROUPS)
    mixed = jnp.einsum('gpq,bnqgd->bnpgd', w_s, vb) + b_s.T[None, None, :, :, None]
    return u * mixed.reshape(bsz, length, GROUP_W)


def mlstm_mixer(lat, cx, ig_bias, fg_bias, norm_w):
    dtype = lat[0].dtype

    def branch(q, k, v, i, f):
        seq = (heads(q, MLSTM_HEADS), heads(k, MLSTM_HEADS), heads(v, MLSTM_HEADS))
        gates = (heads(i.astype(jnp.float32), N_DIR) + ig_bias.astype(jnp.float32),
                 heads(f.astype(jnp.float32), N_DIR) + fg_bias.astype(jnp.float32))
        return seq, gates

    (sx, gx), (sc, gcx) = branch(*lat[:3], *lat[4:]), branch(*cx[:3], *cx[4:])
    bsz = lat[0].shape[0]
    state0 = (jnp.zeros((bsz, MLSTM_HEADS, HEAD_DIM, HEAD_DIM), jnp.float32),
              jnp.zeros((bsz, MLSTM_HEADS, HEAD_DIM), jnp.float32),
              jnp.zeros((bsz, MLSTM_HEADS), jnp.float32))
    hx, hc = bidirectional_prefixed(mlstm_scan, sc, gcx, sx, gx, state0)

    def finish(h, o):
        h = rmsnorm(h, norm_w.reshape(MLSTM_HEADS, HEAD_DIM)).reshape(h.shape[:2] + (GROUP_W,))
        return (jax.nn.sigmoid(o.astype(jnp.float32)) * h).astype(dtype)

    return finish(hx, lat[3]), finish(hc, cx[3])


def token_mixing(hx, hc, w_in, gdn_conv_w, gdn_a_log, gdn_dt_bias, gdn_norm_w, swa_sink,
                 gmlp_w_s, gmlp_b_s, gmlp_norm_w, mlstm_ig_bias, mlstm_fg_bias, mlstm_norm_w):
    zx = split_cols(hx @ w_in)
    zc = split_cols(hc @ w_in)
    a_x, a_c = gdn_mixer(zx[0:6], zc[0:6], gdn_conv_w, gdn_a_log, gdn_dt_bias, gdn_norm_w)
    b_x, b_c = swa_mixer(zx[6:9], zc[6:9], swa_sink)
    c_x = gmlp_mixer(zx[9], zx[10], gmlp_w_s, gmlp_b_s, gmlp_norm_w)
    c_c = gmlp_mixer(zc[9], zc[10], gmlp_w_s, gmlp_b_s, gmlp_norm_w)
    d_x, d_c = mlstm_mixer(zx[11:17], zc[11:17], mlstm_ig_bias, mlstm_fg_bias, mlstm_norm_w)
    mix_x = jnp.concatenate([a_x, b_x, c_x, d_x], axis=-1)
    mix_c = jnp.concatenate([a_c, b_c, c_c, d_c], axis=-1)
    return mix_x, mix_c


def channel_mlp(h, w1, w2):
    return jnp.square(jax.nn.relu(h @ w1)) @ w2


def setup_inputs(seed: int = 0) -> dict:
    key = jax.random.key(seed)
    ks = iter(jax.random.split(key, 32))
    f32 = jnp.float32

    def nrm(shape, s):
        return jax.random.normal(next(ks), shape, f32) * s

    x = nrm((BATCH, SEQ, D_MODEL), 1.0)
    c = nrm((BATCH, D_MODEL), 1.0)
    ctx = nrm((BATCH, CTX_LEN, D_MODEL), 1.0)
    c_ctx = nrm((D_MODEL,), 1.0)
    ada_w = nrm((DEPTH, D_MODEL, N_MOD * D_MODEL), 0.5 * D_MODEL ** -0.5)
    ada_b = nrm((DEPTH, N_MOD * D_MODEL), 0.02)
    norm1_w = 1.0 + nrm((DEPTH, D_MODEL), 0.02)
    norm2_w = 1.0 + nrm((DEPTH, D_MODEL), 0.02)
    w_in = nrm((DEPTH, D_MODEL, D_IN), D_MODEL ** -0.5)
    w_out = nrm((DEPTH, D_MIX, D_MODEL), D_MIX ** -0.5)
    gdn_conv_w = nrm((DEPTH, GDN_CONV, 3 * GROUP_W), GDN_CONV ** -0.5)
    gdn_a_log = jnp.log(jax.random.uniform(next(ks), (DEPTH, N_DIR, GDN_HEADS), f32, 1.0, 16.0))
    dt = jnp.exp(jax.random.uniform(next(ks), (DEPTH, N_DIR, GDN_HEADS), f32,
                                    math.log(1e-3), math.log(1e-1)))
    gdn_dt_bias = dt + jnp.log(-jnp.expm1(-dt))
    gdn_norm_w = 1.0 + nrm((DEPTH, HEAD_DIM), 0.02)
    swa_sink = nrm((DEPTH, SWA_Q_HEADS), 1.0)
    gmlp_w_s = nrm((DEPTH, GMLP_GROUPS, GMLP_CHUNK, GMLP_CHUNK), GMLP_CHUNK ** -0.5)
    gmlp_b_s = 1.0 + nrm((DEPTH, GMLP_GROUPS, GMLP_CHUNK), 0.02)
    gmlp_norm_w = 1.0 + nrm((DEPTH, GROUP_W), 0.02)
    mlstm_ig_bias = nrm((DEPTH, N_DIR, MLSTM_HEADS), 0.1)
    mlstm_fg_bias = 3.0 + 3.0 * jax.random.uniform(next(ks), (DEPTH, N_DIR, MLSTM_HEADS), f32)
    mlstm_norm_w = 1.0 + nrm((DEPTH, GROUP_W), 0.02)
    mlp_w1 = nrm((DEPTH, D_MODEL, D_FF), D_MODEL ** -0.5)
    mlp_w2 = nrm((DEPTH, D_FF, D_MODEL), D_FF ** -0.5)
    final_norm_w = 1.0 + nrm((D_MODEL,), 0.02)
    return {'x': x, 'c': c, 'ctx': ctx, 'c_ctx': c_ctx, 'ada_w': ada_w, 'ada_b': ada_b,
            'norm1_w': norm1_w, 'norm2_w': norm2_w, 'w_in': w_in, 'w_out': w_out,
            'gdn_conv_w': gdn_conv_w, 'gdn_a_log': gdn_a_log, 'gdn_dt_bias': gdn_dt_bias,
            'gdn_norm_w': gdn_norm_w, 'swa_sink': swa_sink, 'gmlp_w_s': gmlp_w_s, 'gmlp_b_s': gmlp_b_s,
            'gmlp_norm_w': gmlp_norm_w, 'mlstm_ig_bias': mlstm_ig_bias, 'mlstm_fg_bias': mlstm_fg_bias,
            'mlstm_norm_w': mlstm_norm_w, 'mlp_w1': mlp_w1, 'mlp_w2': mlp_w2, 'final_norm_w': final_norm_w}


def reference(x, c, ctx, c_ctx, ada_w, ada_b, norm1_w, norm2_w, w_in, w_out, gdn_conv_w, gdn_a_log,
              gdn_dt_bias, gdn_norm_w, swa_sink, gmlp_w_s, gmlp_b_s, gmlp_norm_w, mlstm_ig_bias,
              mlstm_fg_bias, mlstm_norm_w, mlp_w1, mlp_w2, final_norm_w):
    for l in range(DEPTH):
        last = l == DEPTH - 1
        mod_x = jnp.split((jax.nn.silu(c) @ ada_w[l] + ada_b[l])[:, None, :], N_MOD, axis=-1)
        mod_c = jnp.split((jax.nn.silu(c_ctx) @ ada_w[l] + ada_b[l])[None, None, :], N_MOD, axis=-1)
        hx = modulate(rmsnorm(x, norm1_w[l]), mod_x[0], mod_x[1])
        hc = modulate(rmsnorm(ctx, norm1_w[l]), mod_c[0], mod_c[1])
        mix_x, mix_c = token_mixing(hx, hc, w_in[l], gdn_conv_w[l], gdn_a_log[l], gdn_dt_bias[l],
                                    gdn_norm_w[l], swa_sink[l], gmlp_w_s[l], gmlp_b_s[l], gmlp_norm_w[l],
                                    mlstm_ig_bias[l], mlstm_fg_bias[l], mlstm_norm_w[l])
        x = x + mod_x[2] * (mix_x @ w_out[l])
        x = x + mod_x[5] * channel_mlp(modulate(rmsnorm(x, norm2_w[l]), mod_x[3], mod_x[4]),
                                       mlp_w1[l], mlp_w2[l])
        if not last:
            ctx = ctx + mod_c[2] * (mix_c @ w_out[l])
            ctx = ctx + mod_c[5] * channel_mlp(modulate(rmsnorm(ctx, norm2_w[l]), mod_c[3], mod_c[4]),
                                               mlp_w1[l], mlp_w2[l])
    return rmsnorm(x, final_norm_w)
```

```python
import functools

import jax
import jax.numpy as jnp
from jax import lax
from jax.experimental import pallas as pl
from jax.experimental.pallas import tpu as pltpu

F32 = jnp.float32
BF16 = jnp.bfloat16

HEAD_DIM = 64
N_DIR = 2
N_MOD = 6
NORM_EPS = 1e-6
SCAN_CHUNK = 64
GDN_CONV = 5
SWA_KV_HEADS = 2
SWA_WINDOW = 128
SWA_BLOCK = 128
GMLP_CHUNK = 128
GRID_W = 64
ROPE_THETA = 10000.0

LANES = 128
SUBLANES = 8
GATE_W = LANES
NEG_BIG = -1e30
VMEM_LIMIT = 56 * 1024 * 1024


def _dot(a, b):
    return jnp.dot(a.astype(BF16), b.astype(BF16), preferred_element_type=F32)


def _dot_nt(a, b):
    return lax.dot_general(a.astype(BF16), b.astype(BF16), (((1,), (1,)), ((), ())),
                           preferred_element_type=F32)


def _dot_tn(a, b):
    return lax.dot_general(a.astype(BF16), b.astype(BF16), (((0,), (0,)), ((), ())),
                           preferred_element_type=F32)


def _dot_f32(a, b):
    return jnp.dot(a, b, preferred_element_type=F32, precision=lax.Precision.HIGHEST)


def _params(*sem):
    return pltpu.CompilerParams(dimension_semantics=sem, vmem_limit_bytes=VMEM_LIMIT)


def _rms(x, n):
    return x * lax.rsqrt(jnp.sum(x * x, axis=-1, keepdims=True) * (1.0 / n) + NORM_EPS)


def _tri_masks(c):
    row = lax.broadcasted_iota(jnp.int32, (c, c), 0)
    col = lax.broadcasted_iota(jnp.int32, (c, c), 1)
    incl = (row >= col, row <= col)
    strict = (row > col, row < col)
    return incl, strict


def _scan_chunk_index(i, d, ncc, nc):
    if d == 0:
        return i
    return jnp.where(i < ncc, ncc - 1 - i, ncc + nc - 1 - i)


def _adaln_kernel(c_ref, w_ref, b_ref, o_ref):
    o_ref[...] = _dot(jax.nn.silu(c_ref[...]), w_ref[...]) + b_ref[...]


def _adaln(cvec, ada_w, ada_b):
    depth, d, n = ada_w.shape
    rows = cvec.shape[0]
    bn = n // N_MOD
    return pl.pallas_call(
        _adaln_kernel,
        grid=(depth, n // bn),
        in_specs=[pl.BlockSpec((rows, d), lambda l, j: (0, 0)),
                  pl.BlockSpec((None, d, bn), lambda l, j: (l, 0, j)),
                  pl.BlockSpec((None, 1, bn), lambda l, j: (l, 0, j))],
        out_specs=pl.BlockSpec((None, rows, bn), lambda l, j: (l, 0, j)),
        out_shape=jax.ShapeDtypeStruct((depth, rows, n), F32),
        compiler_params=_params("arbitrary", "arbitrary"),
        name="adaln",
    )(cvec, ada_w, ada_b.reshape(depth, 1, n))


def _inproj_kernel(x_ref, mod_ref, nw_ref, w_ref, z_ref):
    x = x_ref[...]
    h = _rms(x, x.shape[-1]) * nw_ref[...]
    h = h * (1.0 + mod_ref[1:2, :]) + mod_ref[0:1, :]
    z_ref[...] = _dot(h, w_ref[...])


def _inproj(xs, mod, norm_w, w_in, layer, lc, tm):
    b, t, d = xs.shape
    n = w_in.shape[-1]
    nct = lc // tm
    return pl.pallas_call(
        _inproj_kernel,
        grid=(b, t // tm),
        in_specs=[pl.BlockSpec((None, tm, d), lambda bi, i: (bi, i, 0)),
                  pl.BlockSpec((None, None, None, N_MOD, d),
                               lambda bi, i: (layer, bi, jnp.where(i >= nct, 1, 0), 0, 0)),
                  pl.BlockSpec((None, 1, d), lambda bi, i: (layer, 0, 0)),
                  pl.BlockSpec((None, d, n), lambda bi, i: (layer, 0, 0))],
        out_specs=pl.BlockSpec((None, tm, n), lambda bi, i: (bi, i, 0)),
        out_shape=jax.ShapeDtypeStruct((b, t, n), F32),
        compiler_params=_params("arbitrary", "arbitrary"),
        name="inproj",
    )(xs, mod, norm_w, w_in)


def _neumann_inverse(a, eye):
    c = a.shape[0]
    n = -a
    p = eye + n
    span = 2
    while span < c:
        n = _dot_f32(n, n)
        p = p + _dot_f32(p, n)
        span *= 2
    return p


def _gdn_kernel(alog_ref, gb_ref, q_ref, k_ref, v_ref, z_ref, g_ref, cw_ref, nw_ref, o_ref,
                of_ref, ob_ref, s_ref, *, lc, t):
    c = SCAN_CHUNK
    gw = q_ref.shape[-1]
    nh = gw // HEAD_DIM
    nc, ncc = t // c, lc // c
    incl, strict = _tri_masks(c)
    inclf = [m.astype(F32) for m in incl]
    eye = (incl[0] & incl[1]).astype(F32)
    neg_a = -jnp.exp(alog_ref[...])
    s_ref[...] = jnp.zeros_like(s_ref)

    def prep(ci):
        r0 = pl.multiple_of(ci * c, c)
        lo = pl.multiple_of(jnp.maximum(r0 - SUBLANES, 0), SUBLANES)
        hi = pl.multiple_of(jnp.minimum(r0 + c, t - SUBLANES), SUBLANES)
        tpos = r0 + lax.broadcasted_iota(jnp.int32, (c, 1), 0)
        seg_lo = jnp.where(tpos < lc, 0, lc)
        seg_hi = jnp.where(tpos < lc, lc, t)
        half = GDN_CONV // 2
        outs = []
        for idx, ref in enumerate((q_ref, k_ref, v_ref)):
            x = jnp.concatenate([ref[pl.ds(lo, SUBLANES), :], ref[pl.ds(r0, c), :],
                                 ref[pl.ds(hi, SUBLANES), :]], axis=0)
            acc = None
            for tap in range(GDN_CONV):
                dlt = tap - half
                xs = x[SUBLANES + dlt:SUBLANES + dlt + c]
                if dlt != 0:
                    ok = jnp.where((tpos + dlt >= seg_lo) & (tpos + dlt < seg_hi), 1.0, 0.0)
                    xs = jnp.where(ok > 0.5, xs, 0.0)
                term = xs * cw_ref[tap:tap + 1, idx * gw:(idx + 1) * gw]
                acc = term if acc is None else acc + term
            outs.append(jax.nn.silu(acc))
        qs, ks, vs = [], [], []
        for h in range(nh):
            sl = slice(h * HEAD_DIM, (h + 1) * HEAD_DIM)
            qq, kk = outs[0][:, sl], outs[1][:, sl]
            qs.append(qq * lax.rsqrt(jnp.sum(qq * qq, axis=-1, keepdims=True) + NORM_EPS)
                      * HEAD_DIM ** -0.5)
            ks.append(kk * lax.rsqrt(jnp.sum(kk * kk, axis=-1, keepdims=True) + NORM_EPS))
            vs.append(outs[2][:, sl])
        return r0, qs, ks, vs, g_ref[pl.ds(r0, c), :]

    def step(i, carry):
        for d in range(N_DIR):
            r0, qs, ks, vs, gt = prep(_scan_chunk_index(i, d, ncc, nc))
            gval = neg_a * jax.nn.softplus(gt + gb_ref[...])
            beta = jax.nn.sigmoid(gt)
            gcum = _dot_f32(inclf[d], gval)
            gtot = gcum[c - 1:c, :] if d == 0 else gcum[0:1, :]
            gcum_t = gcum.T
            egc = jnp.exp(gcum)
            ekd = jnp.exp(gtot - gcum)
            egl = jnp.exp(gtot)
            o_heads = []
            for h in range(nh):
                col = d * nh + h
                bcol = N_DIR * nh + col
                decay = jnp.exp(jnp.where(incl[d], gcum[:, col:col + 1] - gcum_t[col:col + 1, :],
                                          NEG_BIG))
                kb = ks[h] * beta[:, bcol:bcol + 1]
                both = _dot_nt(jnp.concatenate([kb, qs[h]], axis=0), ks[h])
                a = jnp.where(strict[d], both[:c] * decay, 0.0)
                attn = both[c:] * decay
                tinv = _neumann_inverse(a, eye)
                rhs = jnp.concatenate([vs[h] * beta[:, bcol:bcol + 1], kb * egc[:, col:col + 1]],
                                      axis=1)
                sol = _dot_f32(tinv, rhs)
                u, w = sol[:, :HEAD_DIM], sol[:, HEAD_DIM:]
                state = s_ref[col]
                ws = _dot(jnp.concatenate([w, qs[h] * egc[:, col:col + 1]], axis=0), state)
                v_new = u - ws[:c]
                o_heads.append(ws[c:] + _dot(attn, v_new))
                s_ref[col] = state * egl[:, col:col + 1] + _dot_tn(ks[h] * ekd[:, col:col + 1], v_new)
            o_dir = jnp.concatenate(o_heads, axis=1)
            if d == 0:
                of_ref[pl.ds(r0, c), :] = o_dir
            else:
                ob_ref[pl.ds(r0, c), :] = o_dir
        return carry

    lax.fori_loop(0, nc, step, 0)

    rb = 2 * LANES

    def finish(r, carry):
        r0 = pl.multiple_of(r * rb, rb)
        o = of_ref[pl.ds(r0, rb), :] + ob_ref[pl.ds(r0, rb), :]
        parts = []
        for h in range(nh):
            parts.append(_rms(o[:, h * HEAD_DIM:(h + 1) * HEAD_DIM], HEAD_DIM) * nw_ref[...])
        o_ref[pl.ds(r0, rb), :] = jnp.concatenate(parts, axis=1) * jax.nn.silu(z_ref[pl.ds(r0, rb), :])
        return carry

    lax.fori_loop(0, t // rb, finish, 0)


def _gdn(z, alog_vec, gate_bias, conv_w, norm_w, layer, lc, gw, gate_blk):
    b, t, _ = z.shape
    nh = gw // HEAD_DIM
    col = lambda j: pl.BlockSpec((None, t, gw), lambda bi: (bi, 0, j))
    return pl.pallas_call(
        functools.partial(_gdn_kernel, lc=lc, t=t),
        grid=(b,),
        in_specs=[pl.BlockSpec((None, 1, GATE_W), lambda bi: (layer, 0, 0)),
                  pl.BlockSpec((None, 1, GATE_W), lambda bi: (layer, 0, 0)),
                  col(0), col(1), col(2), col(3),
                  pl.BlockSpec((None, t, GATE_W), lambda bi: (bi, 0, gate_blk)),
                  pl.BlockSpec((None, GDN_CONV, 3 * gw), lambda bi: (layer, 0, 0)),
                  pl.BlockSpec((None, 1, HEAD_DIM), lambda bi: (layer, 0, 0))],
        out_specs=pl.BlockSpec((None, t, gw), lambda bi: (bi, 0, 0)),
        out_shape=jax.ShapeDtypeStruct((b, t, gw), F32),
        scratch_shapes=[pltpu.VMEM((t, gw), F32), pltpu.VMEM((t, gw), F32),
                        pltpu.VMEM((N_DIR * nh, HEAD_DIM, HEAD_DIM), F32)],
        compiler_params=_params("arbitrary"),
        name="gdn",
    )(alog_vec, gate_bias, z, z, z, z, z, conv_w, norm_w)


def _mlstm_kernel(gb_ref, q_ref, k_ref, v_ref, og_ref, g_ref, nw_ref, o_ref,
                  hf_ref, hb_ref, c_ref, m_ref, *, lc, t):
    c = SCAN_CHUNK
    gw = q_ref.shape[-1]
    nh = gw // HEAD_DIM
    nc, ncc = t // c, lc // c
    incl, _ = _tri_masks(c)
    inclf = [m.astype(F32) for m in incl]
    icol = 2 * N_DIR * nh
    fcol = 3 * N_DIR * nh
    ones_col = (lax.broadcasted_iota(jnp.int32, (c, LANES - HEAD_DIM), 1) == 0).astype(F32)
    c_ref[...] = jnp.zeros_like(c_ref)
    m_ref[...] = jnp.zeros_like(m_ref)

    def step(i, carry):
        for d in range(N_DIR):
            r0 = pl.multiple_of(_scan_chunk_index(i, d, ncc, nc) * c, c)
            gt = g_ref[pl.ds(r0, c), :] + gb_ref[...]
            bcum = _dot_f32(inclf[d], jax.nn.log_sigmoid(gt))
            bcum_t = bcum.T
            gt_t = gt.T
            q_all = q_ref[pl.ds(r0, c), :] * HEAD_DIM ** -0.5
            k_all = k_ref[pl.ds(r0, c), :]
            v_all = v_ref[pl.ds(r0, c), :]
            h_heads = []
            for h in range(nh):
                col = d * nh + h
                sl = slice(h * HEAD_DIM, (h + 1) * HEAD_DIM)
                qh, kh = q_all[:, sl], k_all[:, sl]
                vaug = jnp.concatenate([v_all[:, sl], ones_col], axis=1)
                bi = bcum[:, fcol + col:fcol + col + 1]
                bj = bcum_t[fcol + col:fcol + col + 1, :]
                igi = gt[:, icol + col:icol + col + 1]
                igj = gt_t[icol + col:icol + col + 1, :]
                dmat = jnp.where(incl[d], bi - bj + igj, NEG_BIG)
                dmax = jnp.max(dmat, axis=-1, keepdims=True)
                blast = bi[c - 1:c, :] if d == 0 else bi[0:1, :]
                wst = blast - bi + igi
                wsmax = jnp.max(wst, axis=0, keepdims=True)
                m_old = m_ref[col][:, 0:1]
                inter = bi + m_old
                m_t = jnp.maximum(inter, dmax)
                s = _dot_nt(qh, kh) * jnp.exp(dmat - m_t)
                w_inter = jnp.exp(inter - m_t)
                cmem = c_ref[col]
                tot = w_inter * _dot(qh, cmem) + _dot(s, vaug)
                den = tot[:, HEAD_DIM:HEAD_DIM + 1]
                h_heads.append(tot[:, :HEAD_DIM] / jnp.maximum(jnp.abs(den), jnp.exp(-m_t)))
                m_new = jnp.maximum(blast + m_old, wsmax)
                k_w = kh * jnp.exp(wst - m_new)
                c_ref[col] = jnp.exp(blast + m_old - m_new) * cmem + _dot_tn(k_w, vaug)
                m_ref[col] = jnp.broadcast_to(m_new, (1, LANES))
            h_dir = jnp.concatenate(h_heads, axis=1)
            if d == 0:
                hf_ref[pl.ds(r0, c), :] = h_dir
            else:
                hb_ref[pl.ds(r0, c), :] = h_dir
        return carry

    lax.fori_loop(0, nc, step, 0)

    rb = 2 * LANES

    def finish(r, carry):
        r0 = pl.multiple_of(r * rb, rb)
        hsum = hf_ref[pl.ds(r0, rb), :] + hb_ref[pl.ds(r0, rb), :]
        parts = [_rms(hsum[:, h * HEAD_DIM:(h + 1) * HEAD_DIM], HEAD_DIM) for h in range(nh)]
        o_ref[pl.ds(r0, rb), :] = (jax.nn.sigmoid(og_ref[pl.ds(r0, rb), :])
                                    * (jnp.concatenate(parts, axis=1) * nw_ref[...]))
        return carry

    lax.fori_loop(0, t // rb, finish, 0)


def _mlstm(z, gate_bias, norm_w, layer, lc, gw, first_blk, gate_blk):
    b, t, _ = z.shape
    nh = gw // HEAD_DIM
    col = lambda j: pl.BlockSpec((None, t, gw), lambda bi: (bi, 0, first_blk + j))
    return pl.pallas_call(
        functools.partial(_mlstm_kernel, lc=lc, t=t),
        grid=(b,),
        in_specs=[pl.BlockSpec((None, 1, GATE_W), lambda bi: (layer, 0, 0)),
                  col(0), col(1), col(2), col(3),
                  pl.BlockSpec((None, t, GATE_W), lambda bi: (bi, 0, gate_blk)),
                  pl.BlockSpec((None, 1, gw), lambda bi: (layer, 0, 0))],
        out_specs=pl.BlockSpec((None, t, gw), lambda bi: (bi, 0, 0)),
        out_shape=jax.ShapeDtypeStruct((b, t, gw), F32),
        scratch_shapes=[pltpu.VMEM((t, gw), F32), pltpu.VMEM((t, gw), F32),
                        pltpu.VMEM((N_DIR * nh, HEAD_DIM, LANES), F32),
                        pltpu.VMEM((N_DIR * nh, 1, LANES), F32)],
        compiler_params=_params("arbitrary"),
        name="mlstm",
    )(gate_bias, z, z, z, z, z, norm_w)


def _swa_kernel(sink_ref, q_ref, kv_ref, cos_ref, sin_ref, o_ref, qr_ref, kr_ref, *, lc, t, layer):
    gw = q_ref.shape[-1]
    kvw = SWA_KV_HEADS * HEAD_DIM
    grp = gw // kvw
    s_len = t - lc
    blk = SWA_BLOCK
    band = 3 * blk
    scale = HEAD_DIM ** -0.5
    half = HEAD_DIM // 2

    qr_ref[0:lc, :] = q_ref[0:lc, :] * scale
    kr_ref[0:lc, :] = kv_ref[0:lc, 0:kvw]

    lane = lax.broadcasted_iota(jnp.int32, (1, gw), 1)
    first_half = (lane % HEAD_DIM) < half

    def rope(x, cs, sn):
        w = x.shape[-1]
        rot = jnp.where(first_half[:, :w], pltpu.roll(x, w - half, 1), pltpu.roll(x, half, 1))
        return x * cs + rot * sn

    rb = 2 * LANES

    def rope_rows(r, carry):
        p0 = pl.multiple_of(r * rb, rb)
        r0 = pl.multiple_of(lc + r * rb, rb)
        cs, sn = cos_ref[pl.ds(p0, rb), :], sin_ref[pl.ds(p0, rb), :]
        qr_ref[pl.ds(r0, rb), :] = rope(q_ref[pl.ds(r0, rb), :], cs, sn) * scale
        kr_ref[pl.ds(r0, rb), :] = rope(kv_ref[pl.ds(r0, rb), 0:kvw], cs[:, :kvw], sn[:, :kvw])
        return carry

    lax.fori_loop(0, s_len // rb, rope_rows, 0)

    def sink_col(kvh, rows):
        ridx = lax.broadcasted_iota(jnp.int32, (grp * rows, 1), 0)
        out = jnp.full((grp * rows, 1), sink_ref[layer, kvh * grp], F32)
        for g in range(1, grp):
            out = jnp.where(ridx >= g * rows, sink_ref[layer, kvh * grp + g], out)
        return out

    def stack_heads(qb, kvh):
        return jnp.concatenate([qb[:, (kvh * grp + g) * HEAD_DIM:(kvh * grp + g + 1) * HEAD_DIM]
                                for g in range(grp)], axis=0)

    def unstack_heads(o_list, rows):
        return jnp.concatenate([o[g * rows:(g + 1) * rows] for o in o_list for g in range(grp)], axis=1)

    outs = []
    for kvh in range(SWA_KV_HEADS):
        hs = slice(kvh * HEAD_DIM, (kvh + 1) * HEAD_DIM)
        q2 = stack_heads(qr_ref[0:lc, :], kvh)
        s = _dot_nt(q2, kr_ref[0:lc, hs])
        sk = sink_col(kvh, lc)
        m = jnp.maximum(jnp.max(s, axis=-1, keepdims=True), sk)
        e = jnp.exp(s - m)
        den = jnp.exp(sk - m) + jnp.sum(e, axis=-1, keepdims=True)
        outs.append(_dot(e, kv_ref[0:lc, kvw + kvh * HEAD_DIM:kvw + (kvh + 1) * HEAD_DIM]) / den)
    o_ref[0:lc, :] = unstack_heads(outs, lc)

    def block(n, carry):
        qrow = pl.multiple_of(lc + n * blk, blk)
        boff = pl.multiple_of(jnp.clip((n - 1) * blk, 0, s_len - band), blk)
        start = pl.multiple_of(lc + boff, blk)
        qb = qr_ref[pl.ds(qrow, blk), :]
        kb = kr_ref[pl.ds(start, band), :]
        vb = kv_ref[pl.ds(start, band), kvw:2 * kvw]
        qpos = n * blk + lax.broadcasted_iota(jnp.int32, (grp * blk, band), 0) % blk
        kpos = boff + lax.broadcasted_iota(jnp.int32, (grp * blk, band), 1)
        mask = jnp.abs(qpos - kpos) <= SWA_WINDOW
        outs = []
        for kvh in range(SWA_KV_HEADS):
            hs = slice(kvh * HEAD_DIM, (kvh + 1) * HEAD_DIM)
            q2 = stack_heads(qb, kvh)
            sb = jnp.where(mask, _dot_nt(q2, kb[:, hs]), NEG_BIG)
            sx = _dot_nt(q2, kr_ref[0:lc, hs])
            sk = sink_col(kvh, blk)
            m = jnp.maximum(jnp.maximum(jnp.max(sb, axis=-1, keepdims=True),
                                        jnp.max(sx, axis=-1, keepdims=True)), sk)
            eb = jnp.exp(sb - m)
            ex = jnp.exp(sx - m)
            den = (jnp.exp(sk - m) + jnp.sum(eb, axis=-1, keepdims=True)
                   + jnp.sum(ex, axis=-1, keepdims=True))
            vc = kv_ref[0:lc, kvw + kvh * HEAD_DIM:kvw + (kvh + 1) * HEAD_DIM]
            outs.append((_dot(eb, vb[:, hs]) + _dot(ex, vc)) / den)
        o_ref[pl.ds(qrow, blk), :] = unstack_heads(outs, blk)
        return carry

    lax.fori_loop(0, s_len // blk, block, 0)


def _swa(z, sink, cos_t, sin_t, layer, lc, gw, q_blk, kv_blk):
    b, t, _ = z.shape
    s_len = t - lc
    kvw = SWA_KV_HEADS * HEAD_DIM
    return pl.pallas_call(
        functools.partial(_swa_kernel, lc=lc, t=t, layer=layer),
        grid=(b,),
        in_specs=[pl.BlockSpec(memory_space=pltpu.SMEM),
                  pl.BlockSpec((None, t, gw), lambda bi: (bi, 0, q_blk)),
                  pl.BlockSpec((None, t, 2 * kvw), lambda bi: (bi, 0, kv_blk)),
                  pl.BlockSpec((s_len, gw), lambda bi: (0, 0)),
                  pl.BlockSpec((s_len, gw), lambda bi: (0, 0))],
        out_specs=pl.BlockSpec((None, t, gw), lambda bi: (bi, 0, 0)),
        out_shape=jax.ShapeDtypeStruct((b, t, gw), F32),
        scratch_shapes=[pltpu.VMEM((t, gw), F32), pltpu.VMEM((t, kvw), F32)],
        compiler_params=_params("arbitrary"),
        name="swa",
    )(sink, z, z, cos_t, sin_t)


def _gmlp_kernel(u_ref, v_ref, ws_ref, bs_ref, nw_ref, o_ref, *, t):
    gw = u_ref.shape[-1]
    ng = ws_ref.shape[0]
    cw = gw // ng
    ck = GMLP_CHUNK

    def chunk(ci, carry):
        r0 = pl.multiple_of(ci * ck, ck)
        u = jax.nn.gelu(u_ref[pl.ds(r0, ck), :])
        v = _rms(jax.nn.gelu(v_ref[pl.ds(r0, ck), :]), gw) * nw_ref[...]
        parts = [_dot(ws_ref[g], v[:, g * cw:(g + 1) * cw]) + bs_ref[:, g:g + 1] for g in range(ng)]
        o_ref[pl.ds(r0, ck), :] = u * jnp.concatenate(parts, axis=1)
        return carry

    lax.fori_loop(0, t // ck, chunk, 0)


def _gmlp(z, w_s, b_s_t, norm_w, layer, gw, u_blk):
    b, t, _ = z.shape
    ng = w_s.shape[1]
    return pl.pallas_call(
        functools.partial(_gmlp_kernel, t=t),
        grid=(b,),
        in_specs=[pl.BlockSpec((None, t, gw), lambda bi: (bi, 0, u_blk)),
                  pl.BlockSpec((None, t, gw), lambda bi: (bi, 0, u_blk + 1)),
                  pl.BlockSpec((None, ng, GMLP_CHUNK, GMLP_CHUNK), lambda bi: (layer, 0, 0, 0)),
                  pl.BlockSpec((None, GMLP_CHUNK, ng), lambda bi: (layer, 0, 0)),
                  pl.BlockSpec((None, 1, gw), lambda bi: (layer, 0, 0))],
        out_specs=pl.BlockSpec((None, t, gw), lambda bi: (bi, 0, 0)),
        out_shape=jax.ShapeDtypeStruct((b, t, gw), F32),
        compiler_params=_params("arbitrary"),
        name="gmlp",
    )(z, z, w_s, b_s_t, norm_w)


def _outmlp_kernel(x_ref, a_ref, b_ref, c_ref, d_ref, mod_ref, nw_ref, wo_ref, w1_ref, w2_ref, o_ref,
                   *, ff_blk):
    gw = a_ref.shape[-1]
    acc = None
    for g, ref in enumerate((a_ref, b_ref, c_ref, d_ref)):
        term = _dot(ref[...], wo_ref[g * gw:(g + 1) * gw, :])
        acc = term if acc is None else acc + term
    x1 = x_ref[...] + mod_ref[2:3, :] * acc
    h = _rms(x1, x1.shape[-1]) * nw_ref[...]
    h = (h * (1.0 + mod_ref[4:5, :]) + mod_ref[3:4, :]).astype(BF16)
    y = None
    for j in range(w1_ref.shape[-1] // ff_blk):
        hid = jnp.square(jnp.maximum(_dot(h, w1_ref[:, j * ff_blk:(j + 1) * ff_blk]), 0.0))
        term = _dot(hid, w2_ref[j * ff_blk:(j + 1) * ff_blk, :])
        y = term if y is None else y + term
    o_ref[...] = x1 + mod_ref[5:6, :] * y


def _outmlp(xs, mixes, mod, norm_w, w_out, w1, w2, layer, lc, tm):
    b, t, d = xs.shape
    gw = mixes[0].shape[-1]
    dff = w1.shape[-1]
    nct = lc // tm
    tile = lambda w: pl.BlockSpec((None, tm, w), lambda bi, i: (bi, i, 0))
    const = lambda r, c: pl.BlockSpec((None, r, c), lambda bi, i: (layer, 0, 0),
                                      pipeline_mode=pl.Buffered(1))
    return pl.pallas_call(
        functools.partial(_outmlp_kernel, ff_blk=min(dff, 1024)),
        grid=(b, t // tm),
        in_specs=[tile(d), tile(gw), tile(gw), tile(gw), tile(gw),
                  pl.BlockSpec((None, None, None, N_MOD, d),
                               lambda bi, i: (layer, bi, jnp.where(i >= nct, 1, 0), 0, 0)),
                  pl.BlockSpec((None, 1, d), lambda bi, i: (layer, 0, 0)),
                  const(d, d), const(d, dff), const(dff, d)],
        out_specs=tile(d),
        out_shape=jax.ShapeDtypeStruct((b, t, d), F32),
        compiler_params=_params("arbitrary", "arbitrary"),
        name="outmlp",
    )(xs, *mixes, mod, norm_w, w_out, w1, w2)


def _final_norm_kernel(x_ref, w_ref, o_ref):
    x = x_ref[...]
    o_ref[...] = _rms(x, x.shape[-1]) * w_ref[...]


def _final_norm(xs, w, lc, tm):
    b, t, d = xs.shape
    s_len = t - lc
    off = lc // tm
    return pl.pallas_call(
        _final_norm_kernel,
        grid=(b, s_len // tm),
        in_specs=[pl.BlockSpec((None, tm, d), lambda bi, i: (bi, i + off, 0)),
                  pl.BlockSpec((1, d), lambda bi, i: (0, 0))],
        out_specs=pl.BlockSpec((None, tm, d), lambda bi, i: (bi, i, 0)),
        out_shape=jax.ShapeDtypeStruct((b, s_len, d), F32),
        compiler_params=_params("arbitrary", "arbitrary"),
        name="final_norm",
    )(xs, w.reshape(1, d))


def _rope_tables(s_len, n_heads):
    rows = s_len // GRID_W
    row = jnp.repeat(jnp.arange(rows), GRID_W).astype(F32)
    col = (jnp.arange(rows * GRID_W) % GRID_W).astype(F32)
    n_freq = HEAD_DIM // 4
    inv = jnp.power(ROPE_THETA, -jnp.arange(n_freq, dtype=F32) / n_freq)
    ang = jnp.concatenate([row[:, None] * inv, col[:, None] * inv], axis=-1)
    cos, sin = jnp.cos(ang), jnp.sin(ang)
    return (jnp.tile(jnp.concatenate([cos, cos], axis=-1), (1, n_heads)),
            jnp.tile(jnp.concatenate([-sin, sin], axis=-1), (1, n_heads)))


def _lane_vec(parts, depth):
    out = jnp.zeros((depth, 1, GATE_W), F32)
    for off, val in parts:
        val = val.reshape(depth, 1, -1).astype(F32)
        out = lax.dynamic_update_slice(out, val, (0, 0, off))
    return out


def kernel(x, c, ctx, c_ctx, ada_w, ada_b, norm1_w, norm2_w, w_in, w_out, gdn_conv_w, gdn_a_log,
           gdn_dt_bias, gdn_norm_w, swa_sink, gmlp_w_s, gmlp_b_s, gmlp_norm_w, mlstm_ig_bias,
           mlstm_fg_bias, mlstm_norm_w, mlp_w1, mlp_w2, final_norm_w):
    bsz, s_len, d = x.shape
    lc = ctx.shape[1]
    depth = ada_w.shape[0]
    gw = d // 4
    nh = gw // HEAD_DIM
    ng = N_DIR * nh
    kvw = SWA_KV_HEADS * HEAD_DIM
    tm = 2 * LANES
    assert lc % tm == 0 and s_len % tm == 0 and s_len >= 3 * SWA_BLOCK and 2 * kvw == gw
    assert 4 * ng <= GATE_W

    sizes = (gw, gw, gw, gw, ng, ng, gw, kvw, kvw, gw, gw, gw, gw, gw, gw, ng, ng)
    offs = [0]
    for sz in sizes:
        offs.append(offs[-1] + sz)
    seg = lambda i: w_in[:, :, offs[i]:offs[i + 1]]
    gate_cols = jnp.concatenate([seg(4), seg(5), seg(15), seg(16)], axis=-1)
    gate_cols = jnp.pad(gate_cols, ((0, 0), (0, 0), (0, GATE_W - 4 * ng)))
    w_in_p = jnp.concatenate([seg(0), seg(1), seg(2), seg(3), seg(6), seg(7), seg(8), seg(9), seg(10),
                              seg(11), seg(12), seg(13), seg(14), gate_cols], axis=-1).astype(BF16)
    gate_blk = (w_in_p.shape[-1] - GATE_W) // GATE_W
    w_out_b, w1_b, w2_b = w_out.astype(BF16), mlp_w1.astype(BF16), mlp_w2.astype(BF16)

    alog_vec = _lane_vec([(0, gdn_a_log)], depth)
    gate_bias = _lane_vec([(0, gdn_dt_bias), (2 * ng, mlstm_ig_bias), (3 * ng, mlstm_fg_bias)], depth)
    cos_t, sin_t = _rope_tables(s_len, nh)
    b_s_t = jnp.swapaxes(gmlp_b_s, 1, 2)

    rows = -(-(bsz + 1) // SUBLANES) * SUBLANES
    cvec = jnp.concatenate([c, c_ctx[None, :], jnp.zeros((rows - bsz - 1, d), F32)], axis=0)
    mod_all = _adaln(cvec, ada_w, ada_b)
    mod_x = mod_all[:, :bsz].reshape(depth, bsz, 1, N_MOD, d)
    mod_c = jnp.broadcast_to(mod_all[:, bsz].reshape(depth, 1, 1, N_MOD, d), mod_x.shape)
    mod = jnp.concatenate([mod_c, mod_x], axis=2)

    n1 = norm1_w.reshape(depth, 1, d)
    n2 = norm2_w.reshape(depth, 1, d)
    gdn_nw = gdn_norm_w.reshape(depth, 1, HEAD_DIM)
    gmlp_nw = gmlp_norm_w.reshape(depth, 1, gw)
    mlstm_nw = mlstm_norm_w.reshape(depth, 1, gw)

    xs = jnp.concatenate([ctx, x], axis=1)
    for l in range(depth):
        z = _inproj(xs, mod, n1, w_in_p, l, lc, tm)
        mix_a = _gdn(z, alog_vec, gate_bias, gdn_conv_w, gdn_nw, l, lc, gw, gate_blk)
        mix_b = _swa(z, swa_sink, cos_t, sin_t, l, lc, gw, 4, 5)
        mix_c = _gmlp(z, gmlp_w_s, b_s_t, gmlp_nw, l, gw, 6)
        mix_d = _mlstm(z, gate_bias, mlstm_nw, l, lc, gw, 8, gate_blk)
        xs = _outmlp(xs, (mix_a, mix_b, mix_c, mix_d), mod, n2, w_out_b, w1_b, w2_b, l, lc, tm)
    return _final_norm(xs, final_norm_w, lc, tm)
```

```python
import functools

import jax
import jax.numpy as jnp
from jax import lax
from jax.experimental import pallas as pl
from jax.experimental.pallas import tpu as pltpu

F32 = jnp.float32
BF16 = jnp.bfloat16

HEAD_DIM = 64
N_DIR = 2
N_MOD = 6
NORM_EPS = 1e-6
SCAN_CHUNK = 64
GDN_CONV = 5
SWA_KV_HEADS = 2
SWA_WINDOW = 128
SWA_BLOCK = 128
GMLP_CHUNK = 128
GRID_W = 64
ROPE_THETA = 10000.0

LANES = 128
SUBLANES = 8
GATE_W = LANES
NEG_BIG = -1e30
VMEM_LIMIT = 56 * 1024 * 1024


def _dot(a, b):
    return jnp.dot(a.astype(BF16), b.astype(BF16), preferred_element_type=F32)


def _dot_nt(a, b):
    return lax.dot_general(a.astype(BF16), b.astype(BF16), (((1,), (1,)), ((), ())),
                           preferred_element_type=F32)


def _dot_tn(a, b):
    return lax.dot_general(a.astype(BF16), b.astype(BF16), (((0,), (0,)), ((), ())),
                           preferred_element_type=F32)


def _split_bf16(x, parts):
    out = []
    for _ in range(parts - 1):
        hi = x.astype(BF16)
        out.append(hi)
        x = x - hi.astype(F32)
    out.append(x.astype(BF16))
    return out


def _dot_mask(mask, x):
    acc = None
    for piece in _split_bf16(x, 3):
        term = jnp.dot(mask, piece, preferred_element_type=F32)
        acc = term if acc is None else acc + term
    return acc


def _dot_solve(a, b):
    a_hi, a_lo = _split_bf16(a, 2)
    b_hi, b_lo = _split_bf16(b, 2)
    return (jnp.dot(a_hi, b_hi, preferred_element_type=F32)
            + (jnp.dot(a_hi, b_lo, preferred_element_type=F32)
               + jnp.dot(a_lo, b_hi, preferred_element_type=F32)))


def _params(*sem):
    return pltpu.CompilerParams(dimension_semantics=sem, vmem_limit_bytes=VMEM_LIMIT)


def _rms(x, n):
    return x * lax.rsqrt(jnp.sum(x * x, axis=-1, keepdims=True) * (1.0 / n) + NORM_EPS)


def _tri_masks(c):
    row = lax.broadcasted_iota(jnp.int32, (c, c), 0)
    col = lax.broadcasted_iota(jnp.int32, (c, c), 1)
    incl = (row >= col, row <= col)
    strict = (row > col, row < col)
    return incl, strict


def _scan_chunk_index(i, d, ncc, nc):
    if d == 0:
        return i
    return jnp.where(i < ncc, ncc - 1 - i, ncc + nc - 1 - i)


def _adaln_kernel(c_ref, w_ref, b_ref, o_ref):
    o_ref[...] = _dot(jax.nn.silu(c_ref[...]), w_ref[...]) + b_ref[...]


def _adaln(cvec, ada_w, ada_b):
    depth, d, n = ada_w.shape
    rows = cvec.shape[0]
    bn = n // N_MOD
    return pl.pallas_call(
        _adaln_kernel,
        grid=(depth, n // bn),
        in_specs=[pl.BlockSpec((rows, d), lambda l, j: (0, 0)),
                  pl.BlockSpec((None, d, bn), lambda l, j: (l, 0, j)),
                  pl.BlockSpec((None, 1, bn), lambda l, j: (l, 0, j))],
        out_specs=pl.BlockSpec((None, rows, bn), lambda l, j: (l, 0, j)),
        out_shape=jax.ShapeDtypeStruct((depth, rows, n), F32),
        compiler_params=_params("arbitrary", "arbitrary"),
        name="adaln",
    )(cvec, ada_w, ada_b.reshape(depth, 1, n))


def _inproj_kernel(x_ref, mod_ref, nw_ref, w_ref, z_ref):
    x = x_ref[...]
    h = _rms(x, x.shape[-1]) * nw_ref[...]
    h = h * (1.0 + mod_ref[1:2, :]) + mod_ref[0:1, :]
    z_ref[...] = _dot(h, w_ref[...])


def _inproj(xs, mod, norm_w, w_in, layer, lc, tm):
    b, t, d = xs.shape
    n = w_in.shape[-1]
    nct = lc // tm
    return pl.pallas_call(
        _inproj_kernel,
        grid=(b, t // tm),
        in_specs=[pl.BlockSpec((None, tm, d), lambda bi, i: (bi, i, 0)),
                  pl.BlockSpec((None, None, None, N_MOD, d),
                               lambda bi, i: (layer, bi, jnp.where(i >= nct, 1, 0), 0, 0)),
                  pl.BlockSpec((None, 1, d), lambda bi, i: (layer, 0, 0)),
                  pl.BlockSpec((None, d, n), lambda bi, i: (layer, 0, 0))],
        out_specs=pl.BlockSpec((None, tm, n), lambda bi, i: (bi, i, 0)),
        out_shape=jax.ShapeDtypeStruct((b, t, n), F32),
        compiler_params=_params("arbitrary", "arbitrary"),
        name="inproj",
    )(xs, mod, norm_w, w_in)


def _gdn_kernel(alog_ref, gb_ref, q_ref, k_ref, v_ref, z_ref, g_ref, cw_ref, nw_ref, o_ref,
                of_ref, ob_ref, s_ref, *, lc, t):
    c = SCAN_CHUNK
    gw = q_ref.shape[-1]
    nh = gw // HEAD_DIM
    nc, ncc = t // c, lc // c
    assert c >= 4 and c & (c - 1) == 0
    incl, strict = _tri_masks(c)
    inclb = [jnp.where(m, 1.0, 0.0).astype(BF16) for m in incl]
    eye = jnp.where(incl[0] & incl[1], 1.0, 0.0)
    neg_a = -jnp.exp(alog_ref[...])
    s_ref[...] = jnp.zeros_like(s_ref)

    def prep(ci):
        r0 = pl.multiple_of(ci * c, c)
        lo = pl.multiple_of(jnp.maximum(r0 - SUBLANES, 0), SUBLANES)
        hi = pl.multiple_of(jnp.minimum(r0 + c, t - SUBLANES), SUBLANES)
        tpos = r0 + lax.broadcasted_iota(jnp.int32, (c, 1), 0)
        seg_lo = jnp.where(tpos < lc, 0, lc)
        seg_hi = jnp.where(tpos < lc, lc, t)
        half = GDN_CONV // 2
        outs = []
        for idx, ref in enumerate((q_ref, k_ref, v_ref)):
            x = jnp.concatenate([ref[pl.ds(lo, SUBLANES), :], ref[pl.ds(r0, c), :],
                                 ref[pl.ds(hi, SUBLANES), :]], axis=0)
            acc = None
            for tap in range(GDN_CONV):
                dlt = tap - half
                xs = x[SUBLANES + dlt:SUBLANES + dlt + c]
                if dlt != 0:
                    ok = jnp.where((tpos + dlt >= seg_lo) & (tpos + dlt < seg_hi), 1.0, 0.0)
                    xs = jnp.where(ok > 0.5, xs, 0.0)
                term = xs * cw_ref[tap:tap + 1, idx * gw:(idx + 1) * gw]
                acc = term if acc is None else acc + term
            outs.append(jax.nn.silu(acc))
        qs, ks, vs = [], [], []
        for h in range(nh):
            sl = slice(h * HEAD_DIM, (h + 1) * HEAD_DIM)
            qq, kk = outs[0][:, sl], outs[1][:, sl]
            qs.append(qq * lax.rsqrt(jnp.sum(qq * qq, axis=-1, keepdims=True) + NORM_EPS)
                      * HEAD_DIM ** -0.5)
            ks.append(kk * lax.rsqrt(jnp.sum(kk * kk, axis=-1, keepdims=True) + NORM_EPS))
            vs.append(outs[2][:, sl])
        return r0, qs, ks, vs, g_ref[pl.ds(r0, c), :]

    def step(i, carry):
        chains = []
        rows = []
        for d in range(N_DIR):
            r0, qs, ks, vs, gt = prep(_scan_chunk_index(i, d, ncc, nc))
            rows.append(r0)
            gval = neg_a * jax.nn.softplus(gt + gb_ref[...])
            beta = jax.nn.sigmoid(gt)
            gcum = _dot_mask(inclb[d], gval)
            gtot = gcum[c - 1:c, :] if d == 0 else gcum[0:1, :]
            gcum_t = gcum.T
            egc = jnp.exp(gcum)
            ekd = jnp.exp(gtot - gcum)
            egl = jnp.exp(gtot)
            for h in range(nh):
                col = d * nh + h
                bcol = N_DIR * nh + col
                kb = ks[h] * beta[:, bcol:bcol + 1]
                chains.append(dict(
                    d=d, col=col, q=qs[h], k=ks[h], kb=kb,
                    decay=jnp.exp(jnp.where(incl[d], gcum[:, col:col + 1] - gcum_t[col:col + 1, :],
                                            NEG_BIG)),
                    rhs=jnp.concatenate([vs[h] * beta[:, bcol:bcol + 1], kb * egc[:, col:col + 1]],
                                        axis=1),
                    q_dec=qs[h] * egc[:, col:col + 1], k_dec=ks[h] * ekd[:, col:col + 1],
                    g_last=egl[:, col:col + 1]))
        for ch in chains:
            ch["both"] = _dot_nt(jnp.concatenate([ch["kb"], ch["q"]], axis=0), ch["k"])
        for ch in chains:
            ch["attn"] = ch["both"][c:] * ch["decay"]
            ch["n"] = jnp.where(strict[ch["d"]], -(ch["both"][:c] * ch["decay"]), 0.0)
            ch["p"] = eye + ch["n"]
        for ch in chains:
            ch["n"] = _dot_solve(ch["n"], ch["n"])
        span = 4
        while span < c:
            for ch in chains:
                nxt = _dot_solve(jnp.concatenate([ch["n"], ch["p"]], axis=0), ch["n"])
                ch["p"] = ch["p"] + nxt[c:]
                ch["n"] = nxt[:c]
            span *= 2
        for ch in chains:
            ch["p"] = ch["p"] + _dot_solve(ch["p"], ch["n"])
        for ch in chains:
            ch["sol"] = _dot_solve(ch["p"], ch["rhs"])
        for ch in chains:
            ch["state"] = s_ref[ch["col"]]
            ch["ws"] = _dot(jnp.concatenate([ch["sol"][:, HEAD_DIM:], ch["q_dec"]], axis=0), ch["state"])
        for ch in chains:
            ch["v_new"] = ch["sol"][:, :HEAD_DIM] - ch["ws"][:c]
            ch["o"] = ch["ws"][c:] + _dot(ch["attn"], ch["v_new"])
        for ch in chains:
            s_ref[ch["col"]] = ch["state"] * ch["g_last"] + _dot_tn(ch["k_dec"], ch["v_new"])
        for d, ref in enumerate((of_ref, ob_ref)):
            ref[pl.ds(rows[d], c), :] = jnp.concatenate([ch["o"] for ch in chains if ch["d"] == d], axis=1)
        return carry

    lax.fori_loop(0, nc, step, 0)

    rb = 2 * LANES

    def finish(r, carry):
        r0 = pl.multiple_of(r * rb, rb)
        o = of_ref[pl.ds(r0, rb), :] + ob_ref[pl.ds(r0, rb), :]
        parts = []
        for h in range(nh):
            parts.append(_rms(o[:, h * HEAD_DIM:(h + 1) * HEAD_DIM], HEAD_DIM) * nw_ref[...])
        o_ref[pl.ds(r0, rb), :] = jnp.concatenate(parts, axis=1) * jax.nn.silu(z_ref[pl.ds(r0, rb), :])
        return carry

    lax.fori_loop(0, t // rb, finish, 0)


def _gdn(z, alog_vec, gate_bias, conv_w, norm_w, layer, lc, gw, gate_blk):
    b, t, _ = z.shape
    nh = gw // HEAD_DIM
    col = lambda j: pl.BlockSpec((None, t, gw), lambda bi: (bi, 0, j))
    return pl.pallas_call(
        functools.partial(_gdn_kernel, lc=lc, t=t),
        grid=(b,),
        in_specs=[pl.BlockSpec((None, 1, GATE_W), lambda bi: (layer, 0, 0)),
                  pl.BlockSpec((None, 1, GATE_W), lambda bi: (layer, 0, 0)),
                  col(0), col(1), col(2), col(3),
                  pl.BlockSpec((None, t, GATE_W), lambda bi: (bi, 0, gate_blk)),
                  pl.BlockSpec((None, GDN_CONV, 3 * gw), lambda bi: (layer, 0, 0)),
                  pl.BlockSpec((None, 1, HEAD_DIM), lambda bi: (layer, 0, 0))],
        out_specs=pl.BlockSpec((None, t, gw), lambda bi: (bi, 0, 0)),
        out_shape=jax.ShapeDtypeStruct((b, t, gw), F32),
        scratch_shapes=[pltpu.VMEM((t, gw), F32), pltpu.VMEM((t, gw), F32),
                        pltpu.VMEM((N_DIR * nh, HEAD_DIM, HEAD_DIM), F32)],
        compiler_params=_params("arbitrary"),
        name="gdn",
    )(alog_vec, gate_bias, z, z, z, z, z, conv_w, norm_w)


def _mlstm_kernel(gb_ref, q_ref, k_ref, v_ref, og_ref, g_ref, nw_ref, o_ref,
                  hf_ref, hb_ref, c_ref, m_ref, *, lc, t):
    c = SCAN_CHUNK
    gw = q_ref.shape[-1]
    nh = gw // HEAD_DIM
    nc, ncc = t // c, lc // c
    incl, _ = _tri_masks(c)
    inclb = [jnp.where(m, 1.0, 0.0).astype(BF16) for m in incl]
    icol = 2 * N_DIR * nh
    fcol = 3 * N_DIR * nh
    ones_col = (lax.broadcasted_iota(jnp.int32, (c, LANES - HEAD_DIM), 1) == 0).astype(F32)
    c_ref[...] = jnp.zeros_like(c_ref)
    m_ref[...] = jnp.zeros_like(m_ref)

    def step(i, carry):
        chains = []
        rows = []
        for d in range(N_DIR):
            r0 = pl.multiple_of(_scan_chunk_index(i, d, ncc, nc) * c, c)
            rows.append(r0)
            gt = g_ref[pl.ds(r0, c), :] + gb_ref[...]
            bcum = _dot_mask(inclb[d], jax.nn.log_sigmoid(gt))
            bcum_t = bcum.T
            gt_t = gt.T
            q_all = q_ref[pl.ds(r0, c), :] * HEAD_DIM ** -0.5
            k_all = k_ref[pl.ds(r0, c), :]
            v_all = v_ref[pl.ds(r0, c), :]
            for h in range(nh):
                col = d * nh + h
                sl = slice(h * HEAD_DIM, (h + 1) * HEAD_DIM)
                bi = bcum[:, fcol + col:fcol + col + 1]
                bj = bcum_t[fcol + col:fcol + col + 1, :]
                igi = gt[:, icol + col:icol + col + 1]
                igj = gt_t[icol + col:icol + col + 1, :]
                dmat = jnp.where(incl[d], bi - bj + igj, NEG_BIG)
                blast = bi[c - 1:c, :] if d == 0 else bi[0:1, :]
                wst = blast - bi + igi
                chains.append(dict(
                    d=d, col=col, q=q_all[:, sl], k=k_all[:, sl],
                    vaug=jnp.concatenate([v_all[:, sl], ones_col], axis=1),
                    bi=bi, dmat=dmat, dmax=jnp.max(dmat, axis=-1, keepdims=True), blast=blast,
                    wst=wst, wsmax=jnp.max(wst, axis=0, keepdims=True)))
        for ch in chains:
            ch["qk"] = _dot_nt(ch["q"], ch["k"])
        for ch in chains:
            ch["cmem"] = c_ref[ch["col"]]
            ch["qc"] = _dot(ch["q"], ch["cmem"])
        for ch in chains:
            m_old = m_ref[ch["col"]][:, 0:1]
            inter = ch["bi"] + m_old
            ch["m_t"] = jnp.maximum(inter, ch["dmax"])
            ch["w_inter"] = jnp.exp(inter - ch["m_t"])
            ch["m_new"] = jnp.maximum(ch["blast"] + m_old, ch["wsmax"])
            ch["c_decay"] = jnp.exp(ch["blast"] + m_old - ch["m_new"])
            ch["sv"] = _dot(ch["qk"] * jnp.exp(ch["dmat"] - ch["m_t"]), ch["vaug"])
        for ch in chains:
            k_w = ch["k"] * jnp.exp(ch["wst"] - ch["m_new"])
            c_ref[ch["col"]] = ch["c_decay"] * ch["cmem"] + _dot_tn(k_w, ch["vaug"])
            m_ref[ch["col"]] = jnp.broadcast_to(ch["m_new"], (1, LANES))
        for ch in chains:
            tot = ch["w_inter"] * ch["qc"] + ch["sv"]
            den = tot[:, HEAD_DIM:HEAD_DIM + 1]
            ch["h"] = tot[:, :HEAD_DIM] / jnp.maximum(jnp.abs(den), jnp.exp(-ch["m_t"]))
        for d, ref in enumerate((hf_ref, hb_ref)):
            ref[pl.ds(rows[d], c), :] = jnp.concatenate([ch["h"] for ch in chains if ch["d"] == d], axis=1)
        return carry

    lax.fori_loop(0, nc, step, 0)

    rb = 2 * LANES

    def finish(r, carry):
        r0 = pl.multiple_of(r * rb, rb)
        hsum = hf_ref[pl.ds(r0, rb), :] + hb_ref[pl.ds(r0, rb), :]
        parts = [_rms(hsum[:, h * HEAD_DIM:(h + 1) * HEAD_DIM], HEAD_DIM) for h in range(nh)]
        o_ref[pl.ds(r0, rb), :] = (jax.nn.sigmoid(og_ref[pl.ds(r0, rb), :])
                                    * (jnp.concatenate(parts, axis=1) * nw_ref[...]))
        return carry

    lax.fori_loop(0, t // rb, finish, 0)


def _mlstm(z, gate_bias, norm_w, layer, lc, gw, first_blk, gate_blk):
    b, t, _ = z.shape
    nh = gw // HEAD_DIM
    col = lambda j: pl.BlockSpec((None, t, gw), lambda bi: (bi, 0, first_blk + j))
    return pl.pallas_call(
        functools.partial(_mlstm_kernel, lc=lc, t=t),
        grid=(b,),
        in_specs=[pl.BlockSpec((None, 1, GATE_W), lambda bi: (layer, 0, 0)),
                  col(0), col(1), col(2), col(3),
                  pl.BlockSpec((None, t, GATE_W), lambda bi: (bi, 0, gate_blk)),
                  pl.BlockSpec((None, 1, gw), lambda bi: (layer, 0, 0))],
        out_specs=pl.BlockSpec((None, t, gw), lambda bi: (bi, 0, 0)),
        out_shape=jax.ShapeDtypeStruct((b, t, gw), F32),
        scratch_shapes=[pltpu.VMEM((t, gw), F32), pltpu.VMEM((t, gw), F32),
                        pltpu.VMEM((N_DIR * nh, HEAD_DIM, LANES), F32),
                        pltpu.VMEM((N_DIR * nh, 1, LANES), F32)],
        compiler_params=_params("arbitrary"),
        name="mlstm",
    )(gate_bias, z, z, z, z, z, norm_w)


def _swa_kernel(sink_ref, q_ref, kv_ref, cos_ref, sin_ref, o_ref, qr_ref, kr_ref, *, lc, t, layer):
    gw = q_ref.shape[-1]
    kvw = SWA_KV_HEADS * HEAD_DIM
    grp = gw // kvw
    s_len = t - lc
    blk = SWA_BLOCK
    band = 3 * blk
    scale = HEAD_DIM ** -0.5
    half = HEAD_DIM // 2

    qr_ref[0:lc, :] = q_ref[0:lc, :] * scale
    kr_ref[0:lc, :] = kv_ref[0:lc, 0:kvw]

    lane = lax.broadcasted_iota(jnp.int32, (1, gw), 1)
    first_half = (lane % HEAD_DIM) < half

    def rope(x, cs, sn):
        w = x.shape[-1]
        rot = jnp.where(first_half[:, :w], pltpu.roll(x, w - half, 1), pltpu.roll(x, half, 1))
        return x * cs + rot * sn

    rb = 2 * LANES

    def rope_rows(r, carry):
        p0 = pl.multiple_of(r * rb, rb)
        r0 = pl.multiple_of(lc + r * rb, rb)
        cs, sn = cos_ref[pl.ds(p0, rb), :], sin_ref[pl.ds(p0, rb), :]
        qr_ref[pl.ds(r0, rb), :] = rope(q_ref[pl.ds(r0, rb), :], cs, sn) * scale
        kr_ref[pl.ds(r0, rb), :] = rope(kv_ref[pl.ds(r0, rb), 0:kvw], cs[:, :kvw], sn[:, :kvw])
        return carry

    lax.fori_loop(0, s_len // rb, rope_rows, 0)

    def sink_col(kvh, rows):
        ridx = lax.broadcasted_iota(jnp.int32, (grp * rows, 1), 0)
        out = jnp.full((grp * rows, 1), sink_ref[layer, kvh * grp], F32)
        for g in range(1, grp):
            out = jnp.where(ridx >= g * rows, sink_ref[layer, kvh * grp + g], out)
        return out

    def stack_heads(qb, kvh):
        return jnp.concatenate([qb[:, (kvh * grp + g) * HEAD_DIM:(kvh * grp + g + 1) * HEAD_DIM]
                                for g in range(grp)], axis=0)

    def unstack_heads(o_list, rows):
        return jnp.concatenate([o[g * rows:(g + 1) * rows] for o in o_list for g in range(grp)], axis=1)

    outs = []
    for kvh in range(SWA_KV_HEADS):
        hs = slice(kvh * HEAD_DIM, (kvh + 1) * HEAD_DIM)
        q2 = stack_heads(qr_ref[0:lc, :], kvh)
        s = _dot_nt(q2, kr_ref[0:lc, hs])
        sk = sink_col(kvh, lc)
        m = jnp.maximum(jnp.max(s, axis=-1, keepdims=True), sk)
        e = jnp.exp(s - m)
        den = jnp.exp(sk - m) + jnp.sum(e, axis=-1, keepdims=True)
        outs.append(_dot(e, kv_ref[0:lc, kvw + kvh * HEAD_DIM:kvw + (kvh + 1) * HEAD_DIM]) / den)
    o_ref[0:lc, :] = unstack_heads(outs, lc)

    def block(n, carry):
        qrow = pl.multiple_of(lc + n * blk, blk)
        boff = pl.multiple_of(jnp.clip((n - 1) * blk, 0, s_len - band), blk)
        start = pl.multiple_of(lc + boff, blk)
        qb = qr_ref[pl.ds(qrow, blk), :]
        kb = kr_ref[pl.ds(start, band), :]
        vb = kv_ref[pl.ds(start, band), kvw:2 * kvw]
        qpos = n * blk + lax.broadcasted_iota(jnp.int32, (grp * blk, band), 0) % blk
        kpos = boff + lax.broadcasted_iota(jnp.int32, (grp * blk, band), 1)
        mask = jnp.abs(qpos - kpos) <= SWA_WINDOW
        outs = []
        for kvh in range(SWA_KV_HEADS):
            hs = slice(kvh * HEAD_DIM, (kvh + 1) * HEAD_DIM)
            q2 = stack_heads(qb, kvh)
            sb = jnp.where(mask, _dot_nt(q2, kb[:, hs]), NEG_BIG)
            sx = _dot_nt(q2, kr_ref[0:lc, hs])
            sk = sink_col(kvh, blk)
            m = jnp.maximum(jnp.maximum(jnp.max(sb, axis=-1, keepdims=True),
                                        jnp.max(sx, axis=-1, keepdims=True)), sk)
            eb = jnp.exp(sb - m)
            ex = jnp.exp(sx - m)
            den = (jnp.exp(sk - m) + jnp.sum(eb, axis=-1, keepdims=True)
                   + jnp.sum(ex, axis=-1, keepdims=True))
            vc = kv_ref[0:lc, kvw + kvh * HEAD_DIM:kvw + (kvh + 1) * HEAD_DIM]
            outs.append((_dot(eb, vb[:, hs]) + _dot(ex, vc)) / den)
        o_ref[pl.ds(qrow, blk), :] = unstack_heads(outs, blk)
        return carry

    lax.fori_loop(0, s_len // blk, block, 0)


def _swa(z, sink, cos_t, sin_t, layer, lc, gw, q_blk, kv_blk):
    b, t, _ = z.shape
    s_len = t - lc
    kvw = SWA_KV_HEADS * HEAD_DIM
    return pl.pallas_call(
        functools.partial(_swa_kernel, lc=lc, t=t, layer=layer),
        grid=(b,),
        in_specs=[pl.BlockSpec(memory_space=pltpu.SMEM),
                  pl.BlockSpec((None, t, gw), lambda bi: (bi, 0, q_blk)),
                  pl.BlockSpec((None, t, 2 * kvw), lambda bi: (bi, 0, kv_blk)),
                  pl.BlockSpec((s_len, gw), lambda bi: (0, 0)),
                  pl.BlockSpec((s_len, gw), lambda bi: (0, 0))],
        out_specs=pl.BlockSpec((None, t, gw), lambda bi: (bi, 0, 0)),
        out_shape=jax.ShapeDtypeStruct((b, t, gw), F32),
        scratch_shapes=[pltpu.VMEM((t, gw), F32), pltpu.VMEM((t, kvw), F32)],
        compiler_params=_params("arbitrary"),
        name="swa",
    )(sink, z, z, cos_t, sin_t)


def _gmlp_kernel(u_ref, v_ref, ws_ref, bs_ref, nw_ref, o_ref, *, t):
    gw = u_ref.shape[-1]
    ng = ws_ref.shape[0]
    cw = gw // ng
    ck = GMLP_CHUNK

    def chunk(ci, carry):
        r0 = pl.multiple_of(ci * ck, ck)
        u = jax.nn.gelu(u_ref[pl.ds(r0, ck), :])
        v = _rms(jax.nn.gelu(v_ref[pl.ds(r0, ck), :]), gw) * nw_ref[...]
        parts = [_dot(ws_ref[g], v[:, g * cw:(g + 1) * cw]) + bs_ref[:, g:g + 1] for g in range(ng)]
        o_ref[pl.ds(r0, ck), :] = u * jnp.concatenate(parts, axis=1)
        return carry

    lax.fori_loop(0, t // ck, chunk, 0)


def _gmlp(z, w_s, b_s_t, norm_w, layer, gw, u_blk):
    b, t, _ = z.shape
    ng = w_s.shape[1]
    return pl.pallas_call(
        functools.partial(_gmlp_kernel, t=t),
        grid=(b,),
        in_specs=[pl.BlockSpec((None, t, gw), lambda bi: (bi, 0, u_blk)),
                  pl.BlockSpec((None, t, gw), lambda bi: (bi, 0, u_blk + 1)),
                  pl.BlockSpec((None, ng, GMLP_CHUNK, GMLP_CHUNK), lambda bi: (layer, 0, 0, 0)),
                  pl.BlockSpec((None, GMLP_CHUNK, ng), lambda bi: (layer, 0, 0)),
                  pl.BlockSpec((None, 1, gw), lambda bi: (layer, 0, 0))],
        out_specs=pl.BlockSpec((None, t, gw), lambda bi: (bi, 0, 0)),
        out_shape=jax.ShapeDtypeStruct((b, t, gw), F32),
        compiler_params=_params("arbitrary"),
        name="gmlp",
    )(z, z, w_s, b_s_t, norm_w)


def _outmlp_kernel(x_ref, a_ref, b_ref, c_ref, d_ref, mod_ref, nw_ref, wo_ref, w1_ref, w2_ref, o_ref,
                   *, ff_blk):
    gw = a_ref.shape[-1]
    acc = None
    for g, ref in enumerate((a_ref, b_ref, c_ref, d_ref)):
        term = _dot(ref[...], wo_ref[g * gw:(g + 1) * gw, :])
        acc = term if acc is None else acc + term
    x1 = x_ref[...] + mod_ref[2:3, :] * acc
    h = _rms(x1, x1.shape[-1]) * nw_ref[...]
    h = (h * (1.0 + mod_ref[4:5, :]) + mod_ref[3:4, :]).astype(BF16)
    y = None
    for j in range(w1_ref.shape[-1] // ff_blk):
        hid = jnp.square(jnp.maximum(_dot(h, w1_ref[:, j * ff_blk:(j + 1) * ff_blk]), 0.0))
        term = _dot(hid, w2_ref[j * ff_blk:(j + 1) * ff_blk, :])
        y = term if y is None else y + term
    o_ref[...] = x1 + mod_ref[5:6, :] * y


def _outmlp(xs, mixes, mod, norm_w, w_out, w1, w2, layer, lc, tm):
    b, t, d = xs.shape
    gw = mixes[0].shape[-1]
    dff = w1.shape[-1]
    nct = lc // tm
    tile = lambda w: pl.BlockSpec((None, tm, w), lambda bi, i: (bi, i, 0))
    const = lambda r, c: pl.BlockSpec((None, r, c), lambda bi, i: (layer, 0, 0),
                                      pipeline_mode=pl.Buffered(1))
    return pl.pallas_call(
        functools.partial(_outmlp_kernel, ff_blk=min(dff, 1024)),
        grid=(b, t // tm),
        in_specs=[tile(d), tile(gw), tile(gw), tile(gw), tile(gw),
                  pl.BlockSpec((None, None, None, N_MOD, d),
                               lambda bi, i: (layer, bi, jnp.where(i >= nct, 1, 0), 0, 0)),
                  pl.BlockSpec((None, 1, d), lambda bi, i: (layer, 0, 0)),
                  const(d, d), const(d, dff), const(dff, d)],
        out_specs=tile(d),
        out_shape=jax.ShapeDtypeStruct((b, t, d), F32),
        compiler_params=_params("arbitrary", "arbitrary"),
        name="outmlp",
    )(xs, *mixes, mod, norm_w, w_out, w1, w2)


def _final_norm_kernel(x_ref, w_ref, o_ref):
    x = x_ref[...]
    o_ref[...] = _rms(x, x.shape[-1]) * w_ref[...]


def _final_norm(xs, w, lc, tm):
    b, t, d = xs.shape
    s_len = t - lc
    off = lc // tm
    return pl.pallas_call(
        _final_norm_kernel,
        grid=(b, s_len // tm),
        in_specs=[pl.BlockSpec((None, tm, d), lambda bi, i: (bi, i + off, 0)),
                  pl.BlockSpec((1, d), lambda bi, i: (0, 0))],
        out_specs=pl.BlockSpec((None, tm, d), lambda bi, i: (bi, i, 0)),
        out_shape=jax.ShapeDtypeStruct((b, s_len, d), F32),
        compiler_params=_params("arbitrary", "arbitrary"),
        name="final_norm",
    )(xs, w.reshape(1, d))


def _rope_tables(s_len, n_heads):
    rows = s_len // GRID_W
    row = jnp.repeat(jnp.arange(rows), GRID_W).astype(F32)
    col = (jnp.arange(rows * GRID_W) % GRID_W).astype(F32)
    n_freq = HEAD_DIM // 4
    inv = jnp.power(ROPE_THETA, -jnp.arange(n_freq, dtype=F32) / n_freq)
    ang = jnp.concatenate([row[:, None] * inv, col[:, None] * inv], axis=-1)
    cos, sin = jnp.cos(ang), jnp.sin(ang)
    return (jnp.tile(jnp.concatenate([cos, cos], axis=-1), (1, n_heads)),
            jnp.tile(jnp.concatenate([-sin, sin], axis=-1), (1, n_heads)))


def _lane_vec(parts, depth):
    out = jnp.zeros((depth, 1, GATE_W), F32)
    for off, val in parts:
        val = val.reshape(depth, 1, -1).astype(F32)
        out = lax.dynamic_update_slice(out, val, (0, 0, off))
    return out


def kernel(x, c, ctx, c_ctx, ada_w, ada_b, norm1_w, norm2_w, w_in, w_out, gdn_conv_w, gdn_a_log,
           gdn_dt_bias, gdn_norm_w, swa_sink, gmlp_w_s, gmlp_b_s, gmlp_norm_w, mlstm_ig_bias,
           mlstm_fg_bias, mlstm_norm_w, mlp_w1, mlp_w2, final_norm_w):
    bsz, s_len, d = x.shape
    lc = ctx.shape[1]
    depth = ada_w.shape[0]
    gw = d // 4
    nh = gw // HEAD_DIM
    ng = N_DIR * nh
    kvw = SWA_KV_HEADS * HEAD_DIM
    tm = 2 * LANES
    assert lc % tm == 0 and s_len % tm == 0 and s_len >= 3 * SWA_BLOCK and 2 * kvw == gw
    assert 4 * ng <= GATE_W

    sizes = (gw, gw, gw, gw, ng, ng, gw, kvw, kvw, gw, gw, gw, gw, gw, gw, ng, ng)
    offs = [0]
    for sz in sizes:
        offs.append(offs[-1] + sz)
    seg = lambda i: w_in[:, :, offs[i]:offs[i + 1]]
    gate_cols = jnp.concatenate([seg(4), seg(5), seg(15), seg(16)], axis=-1)
    gate_cols = jnp.pad(gate_cols, ((0, 0), (0, 0), (0, GATE_W - 4 * ng)))
    w_in_p = jnp.concatenate([seg(0), seg(1), seg(2), seg(3), seg(6), seg(7), seg(8), seg(9), seg(10),
                              seg(11), seg(12), seg(13), seg(14), gate_cols], axis=-1).astype(BF16)
    gate_blk = (w_in_p.shape[-1] - GATE_W) // GATE_W
    w_out_b, w1_b, w2_b = w_out.astype(BF16), mlp_w1.astype(BF16), mlp_w2.astype(BF16)

    alog_vec = _lane_vec([(0, gdn_a_log)], depth)
    gate_bias = _lane_vec([(0, gdn_dt_bias), (2 * ng, mlstm_ig_bias), (3 * ng, mlstm_fg_bias)], depth)
    cos_t, sin_t = _rope_tables(s_len, nh)
    b_s_t = jnp.swapaxes(gmlp_b_s, 1, 2)

    rows = -(-(bsz + 1) // SUBLANES) * SUBLANES
    cvec = jnp.concatenate([c, c_ctx[None, :], jnp.zeros((rows - bsz - 1, d), F32)], axis=0)
    mod_all = _adaln(cvec, ada_w, ada_b)
    mod_x = mod_all[:, :bsz].reshape(depth, bsz, 1, N_MOD, d)
    mod_c = jnp.broadcast_to(mod_all[:, bsz].reshape(depth, 1, 1, N_MOD, d), mod_x.shape)
    mod = jnp.concatenate([mod_c, mod_x], axis=2)

    n1 = norm1_w.reshape(depth, 1, d)
    n2 = norm2_w.reshape(depth, 1, d)
    gdn_nw = gdn_norm_w.reshape(depth, 1, HEAD_DIM)
    gmlp_nw = gmlp_norm_w.reshape(depth, 1, gw)
    mlstm_nw = mlstm_norm_w.reshape(depth, 1, gw)

    xs = jnp.concatenate([ctx, x], axis=1)
    for l in range(depth):
        z = _inproj(xs, mod, n1, w_in_p, l, lc, tm)
        mix_a = _gdn(z, alog_vec, gate_bias, gdn_conv_w, gdn_nw, l, lc, gw, gate_blk)
        mix_b = _swa(z, swa_sink, cos_t, sin_t, l, lc, gw, 4, 5)
        mix_c = _gmlp(z, gmlp_w_s, b_s_t, gmlp_nw, l, gw, 6)
        mix_d = _mlstm(z, gate_bias, mlstm_nw, l, lc, gw, 8, gate_blk)
        xs = _outmlp(xs, (mix_a, mix_b, mix_c, mix_d), mod, n2, w_out_b, w1_b, w2_b, l, lc, tm)
    return _final_norm(xs, final_norm_w, lc, tm)
```

```python
import functools

import jax
import jax.numpy as jnp
from jax import lax
from jax.experimental import pallas as pl
from jax.experimental.pallas import tpu as pltpu

F32 = jnp.float32
BF16 = jnp.bfloat16

HEAD_DIM = 64
N_DIR = 2
N_MOD = 6
NORM_EPS = 1e-6
SCAN_CHUNK = 64
GDN_CONV = 5
SWA_KV_HEADS = 2
SWA_WINDOW = 128
SWA_BLOCK = 128
GMLP_CHUNK = 128
GRID_W = 64
ROPE_THETA = 10000.0

LANES = 128
SUBLANES = 8
GATE_W = LANES
NEG_BIG = -1e30
VMEM_LIMIT = 56 * 1024 * 1024


def _dot(a, b):
    return jnp.dot(a.astype(BF16), b.astype(BF16), preferred_element_type=F32)


def _dot_nt(a, b):
    return lax.dot_general(a.astype(BF16), b.astype(BF16), (((1,), (1,)), ((), ())),
                           preferred_element_type=F32)


def _dot_tn(a, b):
    return lax.dot_general(a.astype(BF16), b.astype(BF16), (((0,), (0,)), ((), ())),
                           preferred_element_type=F32)


def _split_bf16(x, parts):
    out = []
    for _ in range(parts - 1):
        hi = x.astype(BF16)
        out.append(hi)
        x = x - hi.astype(F32)
    out.append(x.astype(BF16))
    return out


def _dot_mask(mask, x):
    acc = None
    for piece in _split_bf16(x, 3):
        term = jnp.dot(mask, piece, preferred_element_type=F32)
        acc = term if acc is None else acc + term
    return acc


def _dot_mask_r(x, mask):
    acc = None
    for piece in _split_bf16(x, 3):
        term = jnp.dot(piece, mask, preferred_element_type=F32)
        acc = term if acc is None else acc + term
    return acc


_NN = (((1,), (0,)), ((), ()))
_NT = (((1,), (1,)), ((), ()))
_TN = (((0,), (0,)), ((), ()))


def _dot_solve(a, b, dims=_NN):
    a_hi, a_lo = _split_bf16(a, 2)
    b_hi, b_lo = _split_bf16(b, 2)
    dg = lambda x, y: lax.dot_general(x, y, dims, preferred_element_type=F32)
    return dg(a_hi, b_hi) + (dg(a_hi, b_lo) + dg(a_lo, b_hi))


def _params(*sem):
    return pltpu.CompilerParams(dimension_semantics=sem, vmem_limit_bytes=VMEM_LIMIT)


def _rms(x, n):
    return x * lax.rsqrt(jnp.sum(x * x, axis=-1, keepdims=True) * (1.0 / n) + NORM_EPS)


def _tri_masks(c):
    row = lax.broadcasted_iota(jnp.int32, (c, c), 0)
    col = lax.broadcasted_iota(jnp.int32, (c, c), 1)
    incl = (row >= col, row <= col)
    strict = (row > col, row < col)
    return incl, strict


def _scan_batch(b):
    return 2 if b % 2 == 0 else 1


def _scan_chunk_index(i, d, ncc, nc):
    if d == 0:
        return i
    return jnp.where(i < ncc, ncc - 1 - i, ncc + nc - 1 - i)


def _adaln_kernel(c_ref, w_ref, b_ref, o_ref):
    o_ref[...] = _dot(jax.nn.silu(c_ref[...]), w_ref[...]) + b_ref[...]


def _adaln(cvec, ada_w, ada_b):
    depth, d, n = ada_w.shape
    rows = cvec.shape[0]
    bn = n // N_MOD
    return pl.pallas_call(
        _adaln_kernel,
        grid=(depth, n // bn),
        in_specs=[pl.BlockSpec((rows, d), lambda l, j: (0, 0)),
                  pl.BlockSpec((None, d, bn), lambda l, j: (l, 0, j)),
                  pl.BlockSpec((None, 1, bn), lambda l, j: (l, 0, j))],
        out_specs=pl.BlockSpec((None, rows, bn), lambda l, j: (l, 0, j)),
        out_shape=jax.ShapeDtypeStruct((depth, rows, n), F32),
        compiler_params=_params("arbitrary", "arbitrary"),
        name="adaln",
    )(cvec, ada_w, ada_b.reshape(depth, 1, n))


def _inproj_kernel(x_ref, mod_ref, nw_ref, w_ref, z_ref):
    x = x_ref[...]
    h = _rms(x, x.shape[-1]) * nw_ref[...]
    h = h * (1.0 + mod_ref[1:2, :]) + mod_ref[0:1, :]
    z_ref[...] = _dot(h, w_ref[...])


def _inproj(xs, mod, norm_w, w_in, layer, lc, tm):
    b, t, d = xs.shape
    n = w_in.shape[-1]
    nct = lc // tm
    return pl.pallas_call(
        _inproj_kernel,
        grid=(b, t // tm),
        in_specs=[pl.BlockSpec((None, tm, d), lambda bi, i: (bi, i, 0)),
                  pl.BlockSpec((None, None, None, N_MOD, d),
                               lambda bi, i: (layer, bi, jnp.where(i >= nct, 1, 0), 0, 0)),
                  pl.BlockSpec((None, 1, d), lambda bi, i: (layer, 0, 0)),
                  pl.BlockSpec((None, d, n), lambda bi, i: (layer, 0, 0))],
        out_specs=pl.BlockSpec((None, tm, n), lambda bi, i: (bi, i, 0)),
        out_shape=jax.ShapeDtypeStruct((b, t, n), F32),
        compiler_params=_params("arbitrary", "arbitrary"),
        name="inproj",
    )(xs, mod, norm_w, w_in)


def _gdn_kernel(alog_ref, gb_ref, q_ref, k_ref, v_ref, z_ref, g_ref, cw_ref, nw_ref, o_ref,
                of_ref, ob_ref, s_ref, *, lc, t):
    c = SCAN_CHUNK
    nb = q_ref.shape[0]
    gw = q_ref.shape[-1]
    nh = gw // HEAD_DIM
    nc, ncc = t // c, lc // c
    assert c >= 4 and c & (c - 1) == 0
    incl, strict = _tri_masks(c)
    inclb = [jnp.where(m, 1.0, 0.0).astype(BF16) for m in incl]
    eye = jnp.where(incl[0] & incl[1], 1.0, 0.0)
    neg_a = -jnp.exp(alog_ref[...])
    s_ref[...] = jnp.zeros_like(s_ref)

    def prep(bb, ci):
        r0 = pl.multiple_of(ci * c, c)
        lo = pl.multiple_of(jnp.maximum(r0 - SUBLANES, 0), SUBLANES)
        hi = pl.multiple_of(jnp.minimum(r0 + c, t - SUBLANES), SUBLANES)
        tpos = r0 + lax.broadcasted_iota(jnp.int32, (c, 1), 0)
        seg_lo = jnp.where(tpos < lc, 0, lc)
        seg_hi = jnp.where(tpos < lc, lc, t)
        half = GDN_CONV // 2
        outs = []
        for idx, ref in enumerate((q_ref, k_ref, v_ref)):
            x = jnp.concatenate([ref[bb, pl.ds(lo, SUBLANES), :], ref[bb, pl.ds(r0, c), :],
                                 ref[bb, pl.ds(hi, SUBLANES), :]], axis=0)
            acc = None
            for tap in range(GDN_CONV):
                dlt = tap - half
                xs = x[SUBLANES + dlt:SUBLANES + dlt + c]
                if dlt != 0:
                    ok = jnp.where((tpos + dlt >= seg_lo) & (tpos + dlt < seg_hi), 1.0, 0.0)
                    xs = jnp.where(ok > 0.5, xs, 0.0)
                term = xs * cw_ref[tap:tap + 1, idx * gw:(idx + 1) * gw]
                acc = term if acc is None else acc + term
            outs.append(jax.nn.silu(acc))
        qs, ks, vs = [], [], []
        for h in range(nh):
            sl = slice(h * HEAD_DIM, (h + 1) * HEAD_DIM)
            qq, kk = outs[0][:, sl], outs[1][:, sl]
            qs.append(qq * lax.rsqrt(jnp.sum(qq * qq, axis=-1, keepdims=True) + NORM_EPS)
                      * HEAD_DIM ** -0.5)
            ks.append(kk * lax.rsqrt(jnp.sum(kk * kk, axis=-1, keepdims=True) + NORM_EPS))
            vs.append(outs[2][:, sl])
        return r0, qs, ks, vs, g_ref[bb, pl.ds(r0, c), :]

    def step(i, carry):
        chains = []
        rows = []
        for bb, d in [(bb, d) for bb in range(nb) for d in range(N_DIR)]:
            r0, qs, ks, vs, gt = prep(bb, _scan_chunk_index(i, d, ncc, nc))
            rows.append(r0)
            gval = neg_a * jax.nn.softplus(gt + gb_ref[...])
            beta = jax.nn.sigmoid(gt)
            gcum = _dot_mask(inclb[d], gval)
            gtot = gcum[c - 1:c, :] if d == 0 else gcum[0:1, :]
            gcum_t = gcum.T
            egc = jnp.exp(gcum)
            ekd = jnp.exp(gtot - gcum)
            egl = jnp.exp(gtot)
            for h in range(nh):
                col = d * nh + h
                bcol = N_DIR * nh + col
                kb = ks[h] * beta[:, bcol:bcol + 1]
                chains.append(dict(
                    bb=bb, d=d, slot=bb * N_DIR * nh + col, q=qs[h], k=ks[h], kb=kb,
                    decay=jnp.exp(jnp.where(incl[d], gcum[:, col:col + 1] - gcum_t[col:col + 1, :],
                                            NEG_BIG)),
                    rhs=jnp.concatenate([vs[h] * beta[:, bcol:bcol + 1], kb * egc[:, col:col + 1]],
                                        axis=1),
                    q_dec=qs[h] * egc[:, col:col + 1], k_dec=ks[h] * ekd[:, col:col + 1],
                    g_last=egl[:, col:col + 1]))
        for ch in chains:
            ch["both"] = _dot_solve(jnp.concatenate([ch["kb"], ch["q"]], axis=0), ch["k"], _NT)
        for ch in chains:
            ch["attn"] = ch["both"][c:] * ch["decay"]
            ch["n"] = jnp.where(strict[ch["d"]], -(ch["both"][:c] * ch["decay"]), 0.0)
            ch["p"] = eye + ch["n"]
        for ch in chains:
            ch["n"] = _dot_solve(ch["n"], ch["n"])
        span = 4
        while span < c:
            for ch in chains:
                nxt = _dot_solve(jnp.concatenate([ch["n"], ch["p"]], axis=0), ch["n"])
                ch["p"] = ch["p"] + nxt[c:]
                ch["n"] = nxt[:c]
            span *= 2
        for ch in chains:
            ch["p"] = ch["p"] + _dot_solve(ch["p"], ch["n"])
        for ch in chains:
            ch["sol"] = _dot_solve(ch["p"], ch["rhs"])
        for ch in chains:
            ch["state"] = s_ref[ch["slot"]]
            ch["ws"] = _dot_solve(jnp.concatenate([ch["sol"][:, HEAD_DIM:], ch["q_dec"]], axis=0),
                                  ch["state"])
        for ch in chains:
            ch["v_new"] = ch["sol"][:, :HEAD_DIM] - ch["ws"][:c]
            ch["o"] = ch["ws"][c:] + _dot_solve(ch["attn"], ch["v_new"])
        for ch in chains:
            s_ref[ch["slot"]] = ch["state"] * ch["g_last"] + _dot_solve(ch["k_dec"], ch["v_new"], _TN)
        for bb in range(nb):
            for d, ref in enumerate((of_ref, ob_ref)):
                ref[bb, pl.ds(rows[bb * N_DIR + d], c), :] = jnp.concatenate(
                    [ch["o"] for ch in chains if ch["d"] == d and ch["bb"] == bb], axis=1)
        return carry

    lax.fori_loop(0, nc, step, 0)

    rb = 2 * LANES

    def finish(r, carry):
        r0 = pl.multiple_of(r * rb, rb)
        for bb in range(nb):
            o = of_ref[bb, pl.ds(r0, rb), :] + ob_ref[bb, pl.ds(r0, rb), :]
            parts = []
            for h in range(nh):
                parts.append(_rms(o[:, h * HEAD_DIM:(h + 1) * HEAD_DIM], HEAD_DIM) * nw_ref[...])
            o_ref[bb, pl.ds(r0, rb), :] = (jnp.concatenate(parts, axis=1)
                                           * jax.nn.silu(z_ref[bb, pl.ds(r0, rb), :]))
        return carry

    lax.fori_loop(0, t // rb, finish, 0)


def _gdn(z, alog_vec, gate_bias, conv_w, norm_w, layer, lc, gw, gate_blk):
    b, t, _ = z.shape
    nh = gw // HEAD_DIM
    nb = _scan_batch(b)
    col = lambda j: pl.BlockSpec((nb, t, gw), lambda bi: (bi, 0, j), pipeline_mode=pl.Buffered(1))
    return pl.pallas_call(
        functools.partial(_gdn_kernel, lc=lc, t=t),
        grid=(b // nb,),
        in_specs=[pl.BlockSpec((None, 1, GATE_W), lambda bi: (layer, 0, 0)),
                  pl.BlockSpec((None, 1, GATE_W), lambda bi: (layer, 0, 0)),
                  col(0), col(1), col(2), col(3),
                  pl.BlockSpec((nb, t, GATE_W), lambda bi: (bi, 0, gate_blk),
                               pipeline_mode=pl.Buffered(1)),
                  pl.BlockSpec((None, GDN_CONV, 3 * gw), lambda bi: (layer, 0, 0)),
                  pl.BlockSpec((None, 1, HEAD_DIM), lambda bi: (layer, 0, 0))],
        out_specs=pl.BlockSpec((nb, t, gw), lambda bi: (bi, 0, 0)),
        out_shape=jax.ShapeDtypeStruct((b, t, gw), F32),
        scratch_shapes=[pltpu.VMEM((nb, t, gw), F32), pltpu.VMEM((nb, t, gw), F32),
                        pltpu.VMEM((nb * N_DIR * nh, HEAD_DIM, HEAD_DIM), F32)],
        compiler_params=_params("arbitrary"),
        name="gdn",
    )(alog_vec, gate_bias, z, z, z, z, z, conv_w, norm_w)


def _mlstm_kernel(gb_ref, q_ref, k_ref, v_ref, og_ref, g_ref, nw_ref, o_ref,
                  hf_ref, hb_ref, cn_ref, m_ref, *, lc, t):
    c = SCAN_CHUNK
    nb = q_ref.shape[0]
    gw = q_ref.shape[-1]
    nh = gw // HEAD_DIM
    nc, ncc = t // c, lc // c
    assert c == HEAD_DIM
    icol = 2 * N_DIR * nh
    fcol = 3 * N_DIR * nh

    ii = lax.broadcasted_iota(jnp.int32, (c, gw), 0)
    jj = lax.broadcasted_iota(jnp.int32, (c, gw), 1)
    tri = (ii >= jj % HEAD_DIM, ii <= jj % HEAD_DIM)
    eye4 = jnp.where(ii == jj % HEAD_DIM, 1.0, 0.0)
    ri = lax.broadcasted_iota(jnp.int32, (c, c), 0)
    ci = lax.broadcasted_iota(jnp.int32, (c, c), 1)
    inclb = [jnp.where(ri >= ci, 1.0, 0.0).astype(BF16), jnp.where(ri <= ci, 1.0, 0.0).astype(BF16)]
    ones_cc = jnp.ones((c, c), BF16)
    bi_ = lax.broadcasted_iota(jnp.int32, (gw, gw), 0) // HEAD_DIM
    bj_ = lax.broadcasted_iota(jnp.int32, (gw, gw), 1) // HEAD_DIM
    block = bi_ == bj_
    block_ones = jnp.where(block, 1.0, 0.0).astype(BF16)
    block2 = jnp.concatenate([block, block], axis=1)
    gl = lax.broadcasted_iota(jnp.int32, (GATE_W, gw), 0)
    gh = lax.broadcasted_iota(jnp.int32, (GATE_W, gw), 1) // HEAD_DIM
    sel_f = [jnp.where(gl == fcol + d * nh + gh, 1.0, 0.0).astype(BF16) for d in range(N_DIR)]
    sel_i = [jnp.where(gl == icol + d * nh + gh, 1.0, 0.0).astype(BF16) for d in range(N_DIR)]
    ones_v = jnp.ones((c, gw), F32)

    cn_ref[...] = jnp.zeros_like(cn_ref)
    m_ref[...] = jnp.zeros_like(m_ref)

    def step(i, carry):
        units = [(bb, d) for bb in range(nb) for d in range(N_DIR)]
        es = range(len(units))
        r0 = [pl.multiple_of(_scan_chunk_index(i, d, ncc, nc) * c, c) for _, d in units]
        gt = [g_ref[bb, pl.ds(r0[e], c), :] + gb_ref[...] for e, (bb, _) in enumerate(units)]
        q_all = [q_ref[bb, pl.ds(r0[e], c), :] * HEAD_DIM ** -0.5 for e, (bb, _) in enumerate(units)]
        k_all = [k_ref[bb, pl.ds(r0[e], c), :] for e, (bb, _) in enumerate(units)]
        v_all = [v_ref[bb, pl.ds(r0[e], c), :] for e, (bb, _) in enumerate(units)]
        k_bd = [jnp.where(block, jnp.concatenate([k_all[e]] * nh, axis=0), 0.0) for e in es]
        qk = [_dot_solve(q_all[e], k_bd[e], _NT) for e in es]
        cn = [cn_ref[e] for e in es]
        qcn = [_dot_solve(q_all[e], cn[e]) for e in es]
        bcum = [_dot_mask(inclb[d], jax.nn.log_sigmoid(gt[e])) for e, (_, d) in enumerate(units)]
        b_col = [_dot_mask_r(bcum[e], sel_f[d]) for e, (_, d) in enumerate(units)]
        ig_col = [_dot_mask_r(gt[e], sel_i[d]) for e, (_, d) in enumerate(units)]
        x_col = [ig_col[e] - b_col[e] for e in es]
        x_row = [_dot_mask(ones_cc, x_col[e] * eye4) for e in es]
        b_row = [_dot_mask(ones_cc, b_col[e] * eye4) for e in es]
        m_old = [m_ref[e] for e in es]
        b_last = [b_col[e][c - 1:c, :] if d == 0 else b_col[e][0:1, :] for e, (_, d) in enumerate(units)]
        wst = [b_last[e] - b_col[e] + ig_col[e] for e in es]
        m_new = [jnp.maximum(b_last[e] + m_old[e], jnp.max(wst[e], axis=0, keepdims=True)) for e in es]
        k_w = [k_all[e] * jnp.exp(wst[e] - m_new[e]) for e in es]
        upd = [_dot_solve(k_w[e], jnp.concatenate([v_all[e], ones_v], axis=1), _TN) for e in es]
        for e in es:
            c_decay = jnp.exp(b_last[e] + m_old[e] - m_new[e])
            cn_ref[e] = (jnp.concatenate([c_decay, c_decay], axis=1) * cn[e]
                         + jnp.where(block2, upd[e], 0.0))
            m_ref[e] = m_new[e]
        dmat_t = [jnp.where(tri[1 - d], b_row[e] + x_col[e], NEG_BIG) for e, (_, d) in enumerate(units)]
        mt_row = [jnp.maximum(b_row[e][0:1, :] + m_old[e], jnp.max(dmat_t[e], axis=0, keepdims=True))
                  for e in es]
        m_t = [_dot_mask_r(mt_row[e] * eye4, block_ones) for e in es]
        s = [qk[e] * jnp.exp(jnp.where(tri[d], b_col[e] + x_row[e], NEG_BIG) - m_t[e])
             for e, (_, d) in enumerate(units)]
        v_bd = [jnp.where(block, jnp.concatenate([v_all[e]] * nh, axis=0), 0.0) for e in es]
        sv = [_dot_solve(s[e], jnp.concatenate([v_bd[e], block_ones.astype(F32)], axis=1)) for e in es]
        for e, (bb, d) in enumerate(units):
            ref = hf_ref if d == 0 else hb_ref
            w_inter = jnp.exp(b_col[e] + m_old[e] - m_t[e])
            tot = jnp.concatenate([w_inter, w_inter], axis=1) * qcn[e] + sv[e]
            ref[bb, pl.ds(r0[e], c), :] = (tot[:, :gw]
                                           / jnp.maximum(jnp.abs(tot[:, gw:]), jnp.exp(-m_t[e])))
        return carry

    lax.fori_loop(0, nc, step, 0)

    rb = 2 * LANES

    def finish(r, carry):
        r0 = pl.multiple_of(r * rb, rb)
        for bb in range(nb):
            hsum = hf_ref[bb, pl.ds(r0, rb), :] + hb_ref[bb, pl.ds(r0, rb), :]
            parts = [_rms(hsum[:, h * HEAD_DIM:(h + 1) * HEAD_DIM], HEAD_DIM) for h in range(nh)]
            o_ref[bb, pl.ds(r0, rb), :] = (jax.nn.sigmoid(og_ref[bb, pl.ds(r0, rb), :])
                                           * (jnp.concatenate(parts, axis=1) * nw_ref[...]))
        return carry

    lax.fori_loop(0, t // rb, finish, 0)


def _mlstm(z, gate_bias, norm_w, layer, lc, gw, first_blk, gate_blk):
    b, t, _ = z.shape
    nb = _scan_batch(b)
    col = lambda j: pl.BlockSpec((nb, t, gw), lambda bi: (bi, 0, first_blk + j),
                                 pipeline_mode=pl.Buffered(1))
    return pl.pallas_call(
        functools.partial(_mlstm_kernel, lc=lc, t=t),
        grid=(b // nb,),
        in_specs=[pl.BlockSpec((None, 1, GATE_W), lambda bi: (layer, 0, 0)),
                  col(0), col(1), col(2), col(3),
                  pl.BlockSpec((nb, t, GATE_W), lambda bi: (bi, 0, gate_blk),
                               pipeline_mode=pl.Buffered(1)),
                  pl.BlockSpec((None, 1, gw), lambda bi: (layer, 0, 0))],
        out_specs=pl.BlockSpec((nb, t, gw), lambda bi: (bi, 0, 0)),
        out_shape=jax.ShapeDtypeStruct((b, t, gw), F32),
        scratch_shapes=[pltpu.VMEM((nb, t, gw), F32), pltpu.VMEM((nb, t, gw), F32),
                        pltpu.VMEM((nb * N_DIR, gw, 2 * gw), F32),
                        pltpu.VMEM((nb * N_DIR, 1, gw), F32)],
        compiler_params=_params("arbitrary"),
        name="mlstm",
    )(gate_bias, z, z, z, z, z, norm_w)


def _swa_kernel(sink_ref, q_ref, kv_ref, cos_ref, sin_ref, o_ref, qr_ref, kr_ref, *, lc, t, layer):
    gw = q_ref.shape[-1]
    kvw = SWA_KV_HEADS * HEAD_DIM
    grp = gw // kvw
    s_len = t - lc
    blk = SWA_BLOCK
    band = 3 * blk
    scale = HEAD_DIM ** -0.5
    half = HEAD_DIM // 2

    qr_ref[0:lc, :] = q_ref[0:lc, :] * scale
    kr_ref[0:lc, :] = kv_ref[0:lc, 0:kvw]

    lane = lax.broadcasted_iota(jnp.int32, (1, gw), 1)
    first_half = (lane % HEAD_DIM) < half

    def rope(x, cs, sn):
        w = x.shape[-1]
        rot = jnp.where(first_half[:, :w], pltpu.roll(x, w - half, 1), pltpu.roll(x, half, 1))
        return x * cs + rot * sn

    rb = 2 * LANES

    def rope_rows(r, carry):
        p0 = pl.multiple_of(r * rb, rb)
        r0 = pl.multiple_of(lc + r * rb, rb)
        cs, sn = cos_ref[pl.ds(p0, rb), :], sin_ref[pl.ds(p0, rb), :]
        qr_ref[pl.ds(r0, rb), :] = rope(q_ref[pl.ds(r0, rb), :], cs, sn) * scale
        kr_ref[pl.ds(r0, rb), :] = rope(kv_ref[pl.ds(r0, rb), 0:kvw], cs[:, :kvw], sn[:, :kvw])
        return carry

    lax.fori_loop(0, s_len // rb, rope_rows, 0)

    def sink_col(kvh, rows):
        ridx = lax.broadcasted_iota(jnp.int32, (grp * rows, 1), 0)
        out = jnp.full((grp * rows, 1), sink_ref[layer, kvh * grp], F32)
        for g in range(1, grp):
            out = jnp.where(ridx >= g * rows, sink_ref[layer, kvh * grp + g], out)
        return out

    def stack_heads(qb, kvh):
        return jnp.concatenate([qb[:, (kvh * grp + g) * HEAD_DIM:(kvh * grp + g + 1) * HEAD_DIM]
                                for g in range(grp)], axis=0)

    def unstack_heads(o_list, rows):
        return jnp.concatenate([o[g * rows:(g + 1) * rows] for o in o_list for g in range(grp)], axis=1)

    outs = []
    for kvh in range(SWA_KV_HEADS):
        hs = slice(kvh * HEAD_DIM, (kvh + 1) * HEAD_DIM)
        q2 = stack_heads(qr_ref[0:lc, :], kvh)
        s = _dot_nt(q2, kr_ref[0:lc, hs])
        sk = sink_col(kvh, lc)
        m = jnp.maximum(jnp.max(s, axis=-1, keepdims=True), sk)
        e = jnp.exp(s - m)
        den = jnp.exp(sk - m) + jnp.sum(e, axis=-1, keepdims=True)
        outs.append(_dot(e, kv_ref[0:lc, kvw + kvh * HEAD_DIM:kvw + (kvh + 1) * HEAD_DIM]) / den)
    o_ref[0:lc, :] = unstack_heads(outs, lc)

    def block(n, carry):
        qrow = pl.multiple_of(lc + n * blk, blk)
        boff = pl.multiple_of(jnp.clip((n - 1) * blk, 0, s_len - band), blk)
        start = pl.multiple_of(lc + boff, blk)
        qb = qr_ref[pl.ds(qrow, blk), :]
        kb = kr_ref[pl.ds(start, band), :]
        vb = kv_ref[pl.ds(start, band), kvw:2 * kvw]
        qpos = n * blk + lax.broadcasted_iota(jnp.int32, (grp * blk, band), 0) % blk
        kpos = boff + lax.broadcasted_iota(jnp.int32, (grp * blk, band), 1)
        mask = jnp.abs(qpos - kpos) <= SWA_WINDOW
        outs = []
        for kvh in range(SWA_KV_HEADS):
            hs = slice(kvh * HEAD_DIM, (kvh + 1) * HEAD_DIM)
            q2 = stack_heads(qb, kvh)
            sb = jnp.where(mask, _dot_nt(q2, kb[:, hs]), NEG_BIG)
            sx = _dot_nt(q2, kr_ref[0:lc, hs])
            sk = sink_col(kvh, blk)
            m = jnp.maximum(jnp.maximum(jnp.max(sb, axis=-1, keepdims=True),
                                        jnp.max(sx, axis=-1, keepdims=True)), sk)
            eb = jnp.exp(sb - m)
            ex = jnp.exp(sx - m)
            den = (jnp.exp(sk - m) + jnp.sum(eb, axis=-1, keepdims=True)
                   + jnp.sum(ex, axis=-1, keepdims=True))
            vc = kv_ref[0:lc, kvw + kvh * HEAD_DIM:kvw + (kvh + 1) * HEAD_DIM]
            outs.append((_dot(eb, vb[:, hs]) + _dot(ex, vc)) / den)
        o_ref[pl.ds(qrow, blk), :] = unstack_heads(outs, blk)
        return carry

    lax.fori_loop(0, s_len // blk, block, 0)


def _swa(z, sink, cos_t, sin_t, layer, lc, gw, q_blk, kv_blk):
    b, t, _ = z.shape
    s_len = t - lc
    kvw = SWA_KV_HEADS * HEAD_DIM
    return pl.pallas_call(
        functools.partial(_swa_kernel, lc=lc, t=t, layer=layer),
        grid=(b,),
        in_specs=[pl.BlockSpec(memory_space=pltpu.SMEM),
                  pl.BlockSpec((None, t, gw), lambda bi: (bi, 0, q_blk)),
                  pl.BlockSpec((None, t, 2 * kvw), lambda bi: (bi, 0, kv_blk)),
                  pl.BlockSpec((s_len, gw), lambda bi: (0, 0)),
                  pl.BlockSpec((s_len, gw), lambda bi: (0, 0))],
        out_specs=pl.BlockSpec((None, t, gw), lambda bi: (bi, 0, 0)),
        out_shape=jax.ShapeDtypeStruct((b, t, gw), F32),
        scratch_shapes=[pltpu.VMEM((t, gw), F32), pltpu.VMEM((t, kvw), F32)],
        compiler_params=_params("arbitrary"),
        name="swa",
    )(sink, z, z, cos_t, sin_t)


def _gmlp_kernel(u_ref, v_ref, ws_ref, bs_ref, nw_ref, o_ref, *, t):
    gw = u_ref.shape[-1]
    ng = ws_ref.shape[0]
    cw = gw // ng
    ck = GMLP_CHUNK

    def chunk(ci, carry):
        r0 = pl.multiple_of(ci * ck, ck)
        u = jax.nn.gelu(u_ref[pl.ds(r0, ck), :])
        v = _rms(jax.nn.gelu(v_ref[pl.ds(r0, ck), :]), gw) * nw_ref[...]
        parts = [_dot(ws_ref[g], v[:, g * cw:(g + 1) * cw]) + bs_ref[:, g:g + 1] for g in range(ng)]
        o_ref[pl.ds(r0, ck), :] = u * jnp.concatenate(parts, axis=1)
        return carry

    lax.fori_loop(0, t // ck, chunk, 0)


def _gmlp(z, w_s, b_s_t, norm_w, layer, gw, u_blk):
    b, t, _ = z.shape
    ng = w_s.shape[1]
    return pl.pallas_call(
        functools.partial(_gmlp_kernel, t=t),
        grid=(b,),
        in_specs=[pl.BlockSpec((None, t, gw), lambda bi: (bi, 0, u_blk)),
                  pl.BlockSpec((None, t, gw), lambda bi: (bi, 0, u_blk + 1)),
                  pl.BlockSpec((None, ng, GMLP_CHUNK, GMLP_CHUNK), lambda bi: (layer, 0, 0, 0)),
                  pl.BlockSpec((None, GMLP_CHUNK, ng), lambda bi: (layer, 0, 0)),
                  pl.BlockSpec((None, 1, gw), lambda bi: (layer, 0, 0))],
        out_specs=pl.BlockSpec((None, t, gw), lambda bi: (bi, 0, 0)),
        out_shape=jax.ShapeDtypeStruct((b, t, gw), F32),
        compiler_params=_params("arbitrary"),
        name="gmlp",
    )(z, z, w_s, b_s_t, norm_w)


def _outmlp_kernel(x_ref, a_ref, b_ref, c_ref, d_ref, mod_ref, nw_ref, wo_ref, w1_ref, w2_ref, o_ref,
                   *, ff_blk):
    gw = a_ref.shape[-1]
    acc = None
    for g, ref in enumerate((a_ref, b_ref, c_ref, d_ref)):
        term = _dot(ref[...], wo_ref[g * gw:(g + 1) * gw, :])
        acc = term if acc is None else acc + term
    x1 = x_ref[...] + mod_ref[2:3, :] * acc
    h = _rms(x1, x1.shape[-1]) * nw_ref[...]
    h = (h * (1.0 + mod_ref[4:5, :]) + mod_ref[3:4, :]).astype(BF16)
    y = None
    for j in range(w1_ref.shape[-1] // ff_blk):
        hid = jnp.square(jnp.maximum(_dot(h, w1_ref[:, j * ff_blk:(j + 1) * ff_blk]), 0.0))
        term = _dot(hid, w2_ref[j * ff_blk:(j + 1) * ff_blk, :])
        y = term if y is None else y + term
    o_ref[...] = x1 + mod_ref[5:6, :] * y


def _outmlp(xs, mixes, mod, norm_w, w_out, w1, w2, layer, lc, tm):
    b, t, d = xs.shape
    gw = mixes[0].shape[-1]
    dff = w1.shape[-1]
    nct = lc // tm
    tile = lambda w: pl.BlockSpec((None, tm, w), lambda bi, i: (bi, i, 0))
    const = lambda r, c: pl.BlockSpec((None, r, c), lambda bi, i: (layer, 0, 0),
                                      pipeline_mode=pl.Buffered(1))
    return pl.pallas_call(
        functools.partial(_outmlp_kernel, ff_blk=min(dff, 1024)),
        grid=(b, t // tm),
        in_specs=[tile(d), tile(gw), tile(gw), tile(gw), tile(gw),
                  pl.BlockSpec((None, None, None, N_MOD, d),
                               lambda bi, i: (layer, bi, jnp.where(i >= nct, 1, 0), 0, 0)),
                  pl.BlockSpec((None, 1, d), lambda bi, i: (layer, 0, 0)),
                  const(d, d), const(d, dff), const(dff, d)],
        out_specs=tile(d),
        out_shape=jax.ShapeDtypeStruct((b, t, d), F32),
        compiler_params=_params("arbitrary", "arbitrary"),
        name="outmlp",
    )(xs, *mixes, mod, norm_w, w_out, w1, w2)


def _final_norm_kernel(x_ref, w_ref, o_ref):
    x = x_ref[...]
    o_ref[...] = _rms(x, x.shape[-1]) * w_ref[...]


def _final_norm(xs, w, lc, tm):
    b, t, d = xs.shape
    s_len = t - lc
    off = lc // tm
    return pl.pallas_call(
        _final_norm_kernel,
        grid=(b, s_len // tm),
        in_specs=[pl.BlockSpec((None, tm, d), lambda bi, i: (bi, i + off, 0)),
                  pl.BlockSpec((1, d), lambda bi, i: (0, 0))],
        out_specs=pl.BlockSpec((None, tm, d), lambda bi, i: (bi, i, 0)),
        out_shape=jax.ShapeDtypeStruct((b, s_len, d), F32),
        compiler_params=_params("arbitrary", "arbitrary"),
        name="final_norm",
    )(xs, w.reshape(1, d))


def _rope_tables(s_len, n_heads):
    rows = s_len // GRID_W
    row = jnp.repeat(jnp.arange(rows), GRID_W).astype(F32)
    col = (jnp.arange(rows * GRID_W) % GRID_W).astype(F32)
    n_freq = HEAD_DIM // 4
    inv = jnp.power(ROPE_THETA, -jnp.arange(n_freq, dtype=F32) / n_freq)
    ang = jnp.concatenate([row[:, None] * inv, col[:, None] * inv], axis=-1)
    cos, sin = jnp.cos(ang), jnp.sin(ang)
    return (jnp.tile(jnp.concatenate([cos, cos], axis=-1), (1, n_heads)),
            jnp.tile(jnp.concatenate([-sin, sin], axis=-1), (1, n_heads)))


def _lane_vec(parts, depth):
    out = jnp.zeros((depth, 1, GATE_W), F32)
    for off, val in parts:
        val = val.reshape(depth, 1, -1).astype(F32)
        out = lax.dynamic_update_slice(out, val, (0, 0, off))
    return out


def kernel(x, c, ctx, c_ctx, ada_w, ada_b, norm1_w, norm2_w, w_in, w_out, gdn_conv_w, gdn_a_log,
           gdn_dt_bias, gdn_norm_w, swa_sink, gmlp_w_s, gmlp_b_s, gmlp_norm_w, mlstm_ig_bias,
           mlstm_fg_bias, mlstm_norm_w, mlp_w1, mlp_w2, final_norm_w):
    bsz, s_len, d = x.shape
    lc = ctx.shape[1]
    depth = ada_w.shape[0]
    gw = d // 4
    nh = gw // HEAD_DIM
    ng = N_DIR * nh
    kvw = SWA_KV_HEADS * HEAD_DIM
    tm = 2 * LANES
    assert lc % tm == 0 and s_len % tm == 0 and s_len >= 3 * SWA_BLOCK and 2 * kvw == gw
    assert 4 * ng <= GATE_W

    sizes = (gw, gw, gw, gw, ng, ng, gw, kvw, kvw, gw, gw, gw, gw, gw, gw, ng, ng)
    offs = [0]
    for sz in sizes:
        offs.append(offs[-1] + sz)
    seg = lambda i: w_in[:, :, offs[i]:offs[i + 1]]
    gate_cols = jnp.concatenate([seg(4), seg(5), seg(15), seg(16)], axis=-1)
    gate_cols = jnp.pad(gate_cols, ((0, 0), (0, 0), (0, GATE_W - 4 * ng)))
    w_in_p = jnp.concatenate([seg(0), seg(1), seg(2), seg(3), seg(6), seg(7), seg(8), seg(9), seg(10),
                              seg(11), seg(12), seg(13), seg(14), gate_cols], axis=-1).astype(BF16)
    gate_blk = (w_in_p.shape[-1] - GATE_W) // GATE_W
    w_out_b, w1_b, w2_b = w_out.astype(BF16), mlp_w1.astype(BF16), mlp_w2.astype(BF16)

    alog_vec = _lane_vec([(0, gdn_a_log)], depth)
    gate_bias = _lane_vec([(0, gdn_dt_bias), (2 * ng, mlstm_ig_bias), (3 * ng, mlstm_fg_bias)], depth)
    cos_t, sin_t = _rope_tables(s_len, nh)
    b_s_t = jnp.swapaxes(gmlp_b_s, 1, 2)

    rows = -(-(bsz + 1) // SUBLANES) * SUBLANES
    cvec = jnp.concatenate([c, c_ctx[None, :], jnp.zeros((rows - bsz - 1, d), F32)], axis=0)
    mod_all = _adaln(cvec, ada_w, ada_b)
    mod_x = mod_all[:, :bsz].reshape(depth, bsz, 1, N_MOD, d)
    mod_c = jnp.broadcast_to(mod_all[:, bsz].reshape(depth, 1, 1, N_MOD, d), mod_x.shape)
    mod = jnp.concatenate([mod_c, mod_x], axis=2)

    n1 = norm1_w.reshape(depth, 1, d)
    n2 = norm2_w.reshape(depth, 1, d)
    gdn_nw = gdn_norm_w.reshape(depth, 1, HEAD_DIM)
    gmlp_nw = gmlp_norm_w.reshape(depth, 1, gw)
    mlstm_nw = mlstm_norm_w.reshape(depth, 1, gw)

    xs = jnp.concatenate([ctx, x], axis=1)
    for l in range(depth):
        z = _inproj(xs, mod, n1, w_in_p, l, lc, tm)
        mix_a = _gdn(z, alog_vec, gate_bias, gdn_conv_w, gdn_nw, l, lc, gw, gate_blk)
        mix_b = _swa(z, swa_sink, cos_t, sin_t, l, lc, gw, 4, 5)
        mix_c = _gmlp(z, gmlp_w_s, b_s_t, gmlp_nw, l, gw, 6)
        mix_d = _mlstm(z, gate_bias, mlstm_nw, l, lc, gw, 8, gate_blk)
        xs = _outmlp(xs, (mix_a, mix_b, mix_c, mix_d), mod, n2, w_out_b, w1_b, w2_b, l, lc, tm)
    return _final_norm(xs, final_norm_w, lc, tm)
```

```python
import functools

import jax
import jax.numpy as jnp
from jax import lax
from jax.experimental import pallas as pl
from jax.experimental.pallas import tpu as pltpu

F32 = jnp.float32
BF16 = jnp.bfloat16

HEAD_DIM = 64
N_DIR = 2
N_MOD = 6
NORM_EPS = 1e-6
SCAN_CHUNK = 64
GDN_CONV = 5
SWA_KV_HEADS = 2
SWA_WINDOW = 128
SWA_BLOCK = 128
GMLP_CHUNK = 128
GRID_W = 64
ROPE_THETA = 10000.0

LANES = 128
SUBLANES = 8
GATE_W = LANES
NEG_BIG = -1e30
VMEM_LIMIT = 56 * 1024 * 1024


def _dot(a, b):
    return jnp.dot(a.astype(BF16), b.astype(BF16), preferred_element_type=F32)


def _dot_nt(a, b):
    return lax.dot_general(a.astype(BF16), b.astype(BF16), (((1,), (1,)), ((), ())),
                           preferred_element_type=F32)


def _dot_tn(a, b):
    return lax.dot_general(a.astype(BF16), b.astype(BF16), (((0,), (0,)), ((), ())),
                           preferred_element_type=F32)


def _split_bf16(x, parts):
    out = []
    for _ in range(parts - 1):
        hi = x.astype(BF16)
        out.append(hi)
        x = x - hi.astype(F32)
    out.append(x.astype(BF16))
    return out


def _dot_mask(mask, x):
    acc = None
    for piece in _split_bf16(x, 3):
        term = jnp.dot(mask, piece, preferred_element_type=F32)
        acc = term if acc is None else acc + term
    return acc


def _dot_mask_r(x, mask):
    acc = None
    for piece in _split_bf16(x, 3):
        term = jnp.dot(piece, mask, preferred_element_type=F32)
        acc = term if acc is None else acc + term
    return acc


_NN = (((1,), (0,)), ((), ()))
_NT = (((1,), (1,)), ((), ()))
_TN = (((0,), (0,)), ((), ()))


def _dot_pieces(a, b, dims=_NN):
    dg = lambda x, y: lax.dot_general(x, y, dims, preferred_element_type=F32)
    return dg(a[0], b[0]) + (dg(a[0], b[1]) + dg(a[1], b[0]))


def _dot_solve(a, b, dims=_NN):
    return _dot_pieces(_split_bf16(a, 2), _split_bf16(b, 2), dims)


def _params(*sem):
    return pltpu.CompilerParams(dimension_semantics=sem, vmem_limit_bytes=VMEM_LIMIT)


def _rms(x, n):
    return x * lax.rsqrt(jnp.sum(x * x, axis=-1, keepdims=True) * (1.0 / n) + NORM_EPS)


def _tri_masks(c):
    row = lax.broadcasted_iota(jnp.int32, (c, c), 0)
    col = lax.broadcasted_iota(jnp.int32, (c, c), 1)
    incl = (row >= col, row <= col)
    strict = (row > col, row < col)
    return incl, strict


def _scan_batch(b):
    return 2 if b % 2 == 0 else 1


def _scan_chunk_index(i, d, ncc, nc):
    if d == 0:
        return i
    return jnp.where(i < ncc, ncc - 1 - i, ncc + nc - 1 - i)


def _adaln_kernel(c_ref, w_ref, b_ref, o_ref):
    o_ref[...] = _dot(jax.nn.silu(c_ref[...]), w_ref[...]) + b_ref[...]


def _adaln(cvec, ada_w, ada_b):
    depth, d, n = ada_w.shape
    rows = cvec.shape[0]
    bn = n // N_MOD
    return pl.pallas_call(
        _adaln_kernel,
        grid=(depth, n // bn),
        in_specs=[pl.BlockSpec((rows, d), lambda l, j: (0, 0)),
                  pl.BlockSpec((None, d, bn), lambda l, j: (l, 0, j)),
                  pl.BlockSpec((None, 1, bn), lambda l, j: (l, 0, j))],
        out_specs=pl.BlockSpec((None, rows, bn), lambda l, j: (l, 0, j)),
        out_shape=jax.ShapeDtypeStruct((depth, rows, n), F32),
        compiler_params=_params("arbitrary", "arbitrary"),
        name="adaln",
    )(cvec, ada_w, ada_b.reshape(depth, 1, n))


def _inproj_kernel(x_ref, mod_ref, nw_ref, w_ref, z_ref):
    x = x_ref[...]
    h = _rms(x, x.shape[-1]) * nw_ref[...]
    h = h * (1.0 + mod_ref[1:2, :]) + mod_ref[0:1, :]
    z_ref[...] = _dot(h, w_ref[...])


def _inproj(xs, mod, norm_w, w_in, layer, lc, tm):
    b, t, d = xs.shape
    n = w_in.shape[-1]
    nct = lc // tm
    return pl.pallas_call(
        _inproj_kernel,
        grid=(b, t // tm),
        in_specs=[pl.BlockSpec((None, tm, d), lambda bi, i: (bi, i, 0)),
                  pl.BlockSpec((None, None, None, N_MOD, d),
                               lambda bi, i: (layer, bi, jnp.where(i >= nct, 1, 0), 0, 0)),
                  pl.BlockSpec((None, 1, d), lambda bi, i: (layer, 0, 0)),
                  pl.BlockSpec((None, d, n), lambda bi, i: (layer, 0, 0))],
        out_specs=pl.BlockSpec((None, tm, n), lambda bi, i: (bi, i, 0)),
        out_shape=jax.ShapeDtypeStruct((b, t, n), F32),
        compiler_params=_params("arbitrary", "arbitrary"),
        name="inproj",
    )(xs, mod, norm_w, w_in)


def _gdn_kernel(alog_ref, gb_ref, q_ref, k_ref, v_ref, z_ref, g_ref, cw_ref, nw_ref, o_ref,
                of_ref, ob_ref, s_ref, *, lc, t):
    c = SCAN_CHUNK
    nb = q_ref.shape[0]
    gw = q_ref.shape[-1]
    nh = gw // HEAD_DIM
    nc, ncc = t // c, lc // c
    assert c >= 4 and c & (c - 1) == 0
    incl, strict = _tri_masks(c)
    inclb = [jnp.where(m, 1.0, 0.0).astype(BF16) for m in incl]
    eye = jnp.where(incl[0] & incl[1], 1.0, 0.0)
    neg_a = -jnp.exp(alog_ref[...])
    s_ref[...] = jnp.zeros_like(s_ref)

    def prep(bb, ci):
        r0 = pl.multiple_of(ci * c, c)
        lo = pl.multiple_of(jnp.maximum(r0 - SUBLANES, 0), SUBLANES)
        hi = pl.multiple_of(jnp.minimum(r0 + c, t - SUBLANES), SUBLANES)
        tpos = r0 + lax.broadcasted_iota(jnp.int32, (c, 1), 0)
        seg_lo = jnp.where(tpos < lc, 0, lc)
        seg_hi = jnp.where(tpos < lc, lc, t)
        half = GDN_CONV // 2
        outs = []
        for idx, ref in enumerate((q_ref, k_ref, v_ref)):
            x = jnp.concatenate([ref[bb, pl.ds(lo, SUBLANES), :], ref[bb, pl.ds(r0, c), :],
                                 ref[bb, pl.ds(hi, SUBLANES), :]], axis=0)
            acc = None
            for tap in range(GDN_CONV):
                dlt = tap - half
                xs = x[SUBLANES + dlt:SUBLANES + dlt + c]
                if dlt != 0:
                    ok = jnp.where((tpos + dlt >= seg_lo) & (tpos + dlt < seg_hi), 1.0, 0.0)
                    xs = jnp.where(ok > 0.5, xs, 0.0)
                term = xs * cw_ref[tap:tap + 1, idx * gw:(idx + 1) * gw]
                acc = term if acc is None else acc + term
            outs.append(jax.nn.silu(acc))
        qs, ks, vs = [], [], []
        for h in range(nh):
            sl = slice(h * HEAD_DIM, (h + 1) * HEAD_DIM)
            qq, kk = outs[0][:, sl], outs[1][:, sl]
            qs.append(qq * lax.rsqrt(jnp.sum(qq * qq, axis=-1, keepdims=True) + NORM_EPS)
                      * HEAD_DIM ** -0.5)
            ks.append(kk * lax.rsqrt(jnp.sum(kk * kk, axis=-1, keepdims=True) + NORM_EPS))
            vs.append(outs[2][:, sl])
        return r0, qs, ks, vs, g_ref[bb, pl.ds(r0, c), :]

    def step(i, carry):
        chains = []
        rows = []
        for bb, d in [(bb, d) for bb in range(nb) for d in range(N_DIR)]:
            r0, qs, ks, vs, gt = prep(bb, _scan_chunk_index(i, d, ncc, nc))
            rows.append(r0)
            gval = neg_a * jax.nn.softplus(gt + gb_ref[...])
            beta = jax.nn.sigmoid(gt)
            gcum = _dot_mask(inclb[d], gval)
            gtot = gcum[c - 1:c, :] if d == 0 else gcum[0:1, :]
            gcum_t = gcum.T
            egc = jnp.exp(gcum)
            ekd = jnp.exp(gtot - gcum)
            egl = jnp.exp(gtot)
            for h in range(nh):
                col = d * nh + h
                bcol = N_DIR * nh + col
                kb = ks[h] * beta[:, bcol:bcol + 1]
                chains.append(dict(
                    bb=bb, d=d, slot=bb * N_DIR * nh + col, q=qs[h], k=ks[h], kb=kb,
                    decay=jnp.exp(jnp.where(incl[d], gcum[:, col:col + 1] - gcum_t[col:col + 1, :],
                                            NEG_BIG)),
                    rhs=jnp.concatenate([vs[h] * beta[:, bcol:bcol + 1], kb * egc[:, col:col + 1]],
                                        axis=1),
                    q_dec=qs[h] * egc[:, col:col + 1], k_dec=ks[h] * ekd[:, col:col + 1],
                    g_last=egl[:, col:col + 1]))
        for ch in chains:
            ch["both"] = _dot_solve(jnp.concatenate([ch["kb"], ch["q"]], axis=0), ch["k"], _NT)
        for ch in chains:
            ch["attn"] = ch["both"][c:] * ch["decay"]
            ch["n"] = jnp.where(strict[ch["d"]], -(ch["both"][:c] * ch["decay"]), 0.0)
            ch["p"] = eye + ch["n"]
        for ch in chains:
            ns = _split_bf16(ch["n"], 2)
            ch["n"] = _dot_pieces(ns, ns)
        span = 4
        while span < c:
            for ch in chains:
                ns, ps = _split_bf16(ch["n"], 2), _split_bf16(ch["p"], 2)
                stacked = [jnp.concatenate([ns[i], ps[i]], axis=0) for i in range(2)]
                nxt = _dot_pieces(stacked, ns)
                ch["p"] = ch["p"] + nxt[c:]
                ch["n"] = nxt[:c]
            span *= 2
        for ch in chains:
            ch["p"] = ch["p"] + _dot_solve(ch["p"], ch["n"])
        for ch in chains:
            ch["sol"] = _dot_solve(ch["p"], ch["rhs"])
        for ch in chains:
            ch["state"] = s_ref[ch["slot"]]
            ch["ws"] = _dot(jnp.concatenate([ch["sol"][:, HEAD_DIM:], ch["q_dec"]], axis=0), ch["state"])
        for ch in chains:
            ch["v_new"] = ch["sol"][:, :HEAD_DIM] - ch["ws"][:c]
            ch["o"] = ch["ws"][c:] + _dot(ch["attn"], ch["v_new"])
        for ch in chains:
            s_ref[ch["slot"]] = ch["state"] * ch["g_last"] + _dot_tn(ch["k_dec"], ch["v_new"])
        for bb in range(nb):
            for d, ref in enumerate((of_ref, ob_ref)):
                ref[bb, pl.ds(rows[bb * N_DIR + d], c), :] = jnp.concatenate(
                    [ch["o"] for ch in chains if ch["d"] == d and ch["bb"] == bb], axis=1)
        return carry

    lax.fori_loop(0, nc, step, 0)

    rb = 2 * LANES

    def finish(r, carry):
        r0 = pl.multiple_of(r * rb, rb)
        for bb in range(nb):
            o = of_ref[bb, pl.ds(r0, rb), :] + ob_ref[bb, pl.ds(r0, rb), :]
            parts = []
            for h in range(nh):
                parts.append(_rms(o[:, h * HEAD_DIM:(h + 1) * HEAD_DIM], HEAD_DIM) * nw_ref[...])
            o_ref[bb, pl.ds(r0, rb), :] = (jnp.concatenate(parts, axis=1)
                                           * jax.nn.silu(z_ref[bb, pl.ds(r0, rb), :]))
        return carry

    lax.fori_loop(0, t // rb, finish, 0)


def _gdn(z, alog_vec, gate_bias, conv_w, norm_w, layer, lc, gw, gate_blk):
    b, t, _ = z.shape
    nh = gw // HEAD_DIM
    nb = _scan_batch(b)
    col = lambda j: pl.BlockSpec((nb, t, gw), lambda bi: (bi, 0, j), pipeline_mode=pl.Buffered(1))
    return pl.pallas_call(
        functools.partial(_gdn_kernel, lc=lc, t=t),
        grid=(b // nb,),
        in_specs=[pl.BlockSpec((None, 1, GATE_W), lambda bi: (layer, 0, 0)),
                  pl.BlockSpec((None, 1, GATE_W), lambda bi: (layer, 0, 0)),
                  col(0), col(1), col(2), col(3),
                  pl.BlockSpec((nb, t, GATE_W), lambda bi: (bi, 0, gate_blk),
                               pipeline_mode=pl.Buffered(1)),
                  pl.BlockSpec((None, GDN_CONV, 3 * gw), lambda bi: (layer, 0, 0)),
                  pl.BlockSpec((None, 1, HEAD_DIM), lambda bi: (layer, 0, 0))],
        out_specs=pl.BlockSpec((nb, t, gw), lambda bi: (bi, 0, 0)),
        out_shape=jax.ShapeDtypeStruct((b, t, gw), F32),
        scratch_shapes=[pltpu.VMEM((nb, t, gw), F32), pltpu.VMEM((nb, t, gw), F32),
                        pltpu.VMEM((nb * N_DIR * nh, HEAD_DIM, HEAD_DIM), F32)],
        compiler_params=_params("arbitrary"),
        name="gdn",
    )(alog_vec, gate_bias, z, z, z, z, z, conv_w, norm_w)


def _mlstm_kernel(gb_ref, q_ref, k_ref, v_ref, og_ref, g_ref, nw_ref, o_ref,
                  hf_ref, hb_ref, cn_ref, m_ref, *, lc, t):
    c = SCAN_CHUNK
    nb = q_ref.shape[0]
    gw = q_ref.shape[-1]
    nh = gw // HEAD_DIM
    nc, ncc = t // c, lc // c
    assert c == HEAD_DIM
    icol = 2 * N_DIR * nh
    fcol = 3 * N_DIR * nh

    ii = lax.broadcasted_iota(jnp.int32, (c, gw), 0)
    jj = lax.broadcasted_iota(jnp.int32, (c, gw), 1)
    tri = (ii >= jj % HEAD_DIM, ii <= jj % HEAD_DIM)
    eye4 = jnp.where(ii == jj % HEAD_DIM, 1.0, 0.0)
    ri = lax.broadcasted_iota(jnp.int32, (c, c), 0)
    ci = lax.broadcasted_iota(jnp.int32, (c, c), 1)
    inclb = [jnp.where(ri >= ci, 1.0, 0.0).astype(BF16), jnp.where(ri <= ci, 1.0, 0.0).astype(BF16)]
    ones_cc = jnp.ones((c, c), BF16)
    bi_ = lax.broadcasted_iota(jnp.int32, (gw, gw), 0) // HEAD_DIM
    bj_ = lax.broadcasted_iota(jnp.int32, (gw, gw), 1) // HEAD_DIM
    block = bi_ == bj_
    block_ones = jnp.where(block, 1.0, 0.0).astype(BF16)
    block2 = jnp.concatenate([block, block], axis=1)
    gl = lax.broadcasted_iota(jnp.int32, (GATE_W, gw), 0)
    gh = lax.broadcasted_iota(jnp.int32, (GATE_W, gw), 1) // HEAD_DIM
    sel_f = [jnp.where(gl == fcol + d * nh + gh, 1.0, 0.0).astype(BF16) for d in range(N_DIR)]
    sel_i = [jnp.where(gl == icol + d * nh + gh, 1.0, 0.0).astype(BF16) for d in range(N_DIR)]
    ones_v = jnp.ones((c, gw), F32)

    cn_ref[...] = jnp.zeros_like(cn_ref)
    m_ref[...] = jnp.zeros_like(m_ref)

    def step(i, carry):
        units = [(bb, d) for bb in range(nb) for d in range(N_DIR)]
        es = range(len(units))
        r0 = [pl.multiple_of(_scan_chunk_index(i, d, ncc, nc) * c, c) for _, d in units]
        gt = [g_ref[bb, pl.ds(r0[e], c), :] + gb_ref[...] for e, (bb, _) in enumerate(units)]
        q_all = [q_ref[bb, pl.ds(r0[e], c), :] * HEAD_DIM ** -0.5 for e, (bb, _) in enumerate(units)]
        k_all = [k_ref[bb, pl.ds(r0[e], c), :] for e, (bb, _) in enumerate(units)]
        v_all = [v_ref[bb, pl.ds(r0[e], c), :] for e, (bb, _) in enumerate(units)]
        k_bd = [jnp.where(block, jnp.concatenate([k_all[e]] * nh, axis=0), 0.0) for e in es]
        qk = [_dot_solve(q_all[e], k_bd[e], _NT) for e in es]
        cn = [cn_ref[e] for e in es]
        qcn = [_dot_solve(q_all[e], cn[e]) for e in es]
        bcum = [_dot_mask(inclb[d], jax.nn.log_sigmoid(gt[e])) for e, (_, d) in enumerate(units)]
        b_col = [_dot_mask_r(bcum[e], sel_f[d]) for e, (_, d) in enumerate(units)]
        ig_col = [_dot_mask_r(gt[e], sel_i[d]) for e, (_, d) in enumerate(units)]
        x_col = [ig_col[e] - b_col[e] for e in es]
        x_row = [_dot_mask(ones_cc, x_col[e] * eye4) for e in es]
        b_row = [_dot_mask(ones_cc, b_col[e] * eye4) for e in es]
        m_old = [m_ref[e] for e in es]
        b_last = [b_col[e][c - 1:c, :] if d == 0 else b_col[e][0:1, :] for e, (_, d) in enumerate(units)]
        wst = [b_last[e] - b_col[e] + ig_col[e] for e in es]
        m_new = [jnp.maximum(b_last[e] + m_old[e], jnp.max(wst[e], axis=0, keepdims=True)) for e in es]
        k_w = [k_all[e] * jnp.exp(wst[e] - m_new[e]) for e in es]
        upd = [_dot_tn(k_w[e], jnp.concatenate([v_all[e], ones_v], axis=1)) for e in es]
        for e in es:
            c_decay = jnp.exp(b_last[e] + m_old[e] - m_new[e])
            cn_ref[e] = (jnp.concatenate([c_decay, c_decay], axis=1) * cn[e]
                         + jnp.where(block2, upd[e], 0.0))
            m_ref[e] = m_new[e]
        dmat_t = [jnp.where(tri[1 - d], b_row[e] + x_col[e], NEG_BIG) for e, (_, d) in enumerate(units)]
        mt_row = [jnp.maximum(b_row[e][0:1, :] + m_old[e], jnp.max(dmat_t[e], axis=0, keepdims=True))
                  for e in es]
        m_t = [_dot_mask_r(mt_row[e] * eye4, block_ones) for e in es]
        s = [qk[e] * jnp.exp(jnp.where(tri[d], b_col[e] + x_row[e], NEG_BIG) - m_t[e])
             for e, (_, d) in enumerate(units)]
        v_bd = [jnp.where(block, jnp.concatenate([v_all[e]] * nh, axis=0), 0.0) for e in es]
        sv = [_dot_solve(s[e], jnp.concatenate([v_bd[e], block_ones.astype(F32)], axis=1)) for e in es]
        for e, (bb, d) in enumerate(units):
            ref = hf_ref if d == 0 else hb_ref
            w_inter = jnp.exp(b_col[e] + m_old[e] - m_t[e])
            tot = jnp.concatenate([w_inter, w_inter], axis=1) * qcn[e] + sv[e]
            ref[bb, pl.ds(r0[e], c), :] = (tot[:, :gw]
                                           / jnp.maximum(jnp.abs(tot[:, gw:]), jnp.exp(-m_t[e])))
        return carry

    lax.fori_loop(0, nc, step, 0)

    rb = 2 * LANES

    def finish(r, carry):
        r0 = pl.multiple_of(r * rb, rb)
        for bb in range(nb):
            hsum = hf_ref[bb, pl.ds(r0, rb), :] + hb_ref[bb, pl.ds(r0, rb), :]
            parts = [_rms(hsum[:, h * HEAD_DIM:(h + 1) * HEAD_DIM], HEAD_DIM) for h in range(nh)]
            o_ref[bb, pl.ds(r0, rb), :] = (jax.nn.sigmoid(og_ref[bb, pl.ds(r0, rb), :])
                                           * (jnp.concatenate(parts, axis=1) * nw_ref[...]))
        return carry

    lax.fori_loop(0, t // rb, finish, 0)


def _mlstm(z, gate_bias, norm_w, layer, lc, gw, first_blk, gate_blk):
    b, t, _ = z.shape
    nb = _scan_batch(b)
    col = lambda j: pl.BlockSpec((nb, t, gw), lambda bi: (bi, 0, first_blk + j),
                                 pipeline_mode=pl.Buffered(1))
    return pl.pallas_call(
        functools.partial(_mlstm_kernel, lc=lc, t=t),
        grid=(b // nb,),
        in_specs=[pl.BlockSpec((None, 1, GATE_W), lambda bi: (layer, 0, 0)),
                  col(0), col(1), col(2), col(3),
                  pl.BlockSpec((nb, t, GATE_W), lambda bi: (bi, 0, gate_blk),
                               pipeline_mode=pl.Buffered(1)),
                  pl.BlockSpec((None, 1, gw), lambda bi: (layer, 0, 0))],
        out_specs=pl.BlockSpec((nb, t, gw), lambda bi: (bi, 0, 0)),
        out_shape=jax.ShapeDtypeStruct((b, t, gw), F32),
        scratch_shapes=[pltpu.VMEM((nb, t, gw), F32), pltpu.VMEM((nb, t, gw), F32),
                        pltpu.VMEM((nb * N_DIR, gw, 2 * gw), F32),
                        pltpu.VMEM((nb * N_DIR, 1, gw), F32)],
        compiler_params=_params("arbitrary"),
        name="mlstm",
    )(gate_bias, z, z, z, z, z, norm_w)


def _swa_kernel(sink_ref, q_ref, kv_ref, cos_ref, sin_ref, o_ref, qr_ref, kr_ref, *, lc, t, layer):
    gw = q_ref.shape[-1]
    kvw = SWA_KV_HEADS * HEAD_DIM
    grp = gw // kvw
    s_len = t - lc
    blk = SWA_BLOCK
    band = 3 * blk
    scale = HEAD_DIM ** -0.5
    half = HEAD_DIM // 2

    qr_ref[0:lc, :] = q_ref[0:lc, :] * scale
    kr_ref[0:lc, :] = kv_ref[0:lc, 0:kvw]

    lane = lax.broadcasted_iota(jnp.int32, (1, gw), 1)
    first_half = (lane % HEAD_DIM) < half

    def rope(x, cs, sn):
        w = x.shape[-1]
        rot = jnp.where(first_half[:, :w], pltpu.roll(x, w - half, 1), pltpu.roll(x, half, 1))
        return x * cs + rot * sn

    rb = 2 * LANES

    def rope_rows(r, carry):
        p0 = pl.multiple_of(r * rb, rb)
        r0 = pl.multiple_of(lc + r * rb, rb)
        cs, sn = cos_ref[pl.ds(p0, rb), :], sin_ref[pl.ds(p0, rb), :]
        qr_ref[pl.ds(r0, rb), :] = rope(q_ref[pl.ds(r0, rb), :], cs, sn) * scale
        kr_ref[pl.ds(r0, rb), :] = rope(kv_ref[pl.ds(r0, rb), 0:kvw], cs[:, :kvw], sn[:, :kvw])
        return carry

    lax.fori_loop(0, s_len // rb, rope_rows, 0)

    def sink_col(kvh, rows):
        ridx = lax.broadcasted_iota(jnp.int32, (grp * rows, 1), 0)
        out = jnp.full((grp * rows, 1), sink_ref[layer, kvh * grp], F32)
        for g in range(1, grp):
            out = jnp.where(ridx >= g * rows, sink_ref[layer, kvh * grp + g], out)
        return out

    def stack_heads(qb, kvh):
        return jnp.concatenate([qb[:, (kvh * grp + g) * HEAD_DIM:(kvh * grp + g + 1) * HEAD_DIM]
                                for g in range(grp)], axis=0)

    def unstack_heads(o_list, rows):
        return jnp.concatenate([o[g * rows:(g + 1) * rows] for o in o_list for g in range(grp)], axis=1)

    outs = []
    for kvh in range(SWA_KV_HEADS):
        hs = slice(kvh * HEAD_DIM, (kvh + 1) * HEAD_DIM)
        q2 = stack_heads(qr_ref[0:lc, :], kvh)
        s = _dot_nt(q2, kr_ref[0:lc, hs])
        sk = sink_col(kvh, lc)
        m = jnp.maximum(jnp.max(s, axis=-1, keepdims=True), sk)
        e = jnp.exp(s - m)
        den = jnp.exp(sk - m) + jnp.sum(e, axis=-1, keepdims=True)
        outs.append(_dot(e, kv_ref[0:lc, kvw + kvh * HEAD_DIM:kvw + (kvh + 1) * HEAD_DIM]) / den)
    o_ref[0:lc, :] = unstack_heads(outs, lc)

    def block(n, carry):
        qrow = pl.multiple_of(lc + n * blk, blk)
        boff = pl.multiple_of(jnp.clip((n - 1) * blk, 0, s_len - band), blk)
        start = pl.multiple_of(lc + boff, blk)
        qb = qr_ref[pl.ds(qrow, blk), :]
        kb = kr_ref[pl.ds(start, band), :]
        vb = kv_ref[pl.ds(start, band), kvw:2 * kvw]
        qpos = n * blk + lax.broadcasted_iota(jnp.int32, (grp * blk, band), 0) % blk
        kpos = boff + lax.broadcasted_iota(jnp.int32, (grp * blk, band), 1)
        mask = jnp.abs(qpos - kpos) <= SWA_WINDOW
        kvs = range(SWA_KV_HEADS)
        hs = [slice(kvh * HEAD_DIM, (kvh + 1) * HEAD_DIM) for kvh in kvs]
        q2 = [stack_heads(qb, kvh) for kvh in kvs]
        sb = [jnp.where(mask, _dot_nt(q2[kvh], kb[:, hs[kvh]]), NEG_BIG) for kvh in kvs]
        sx = [_dot_nt(q2[kvh], kr_ref[0:lc, hs[kvh]]) for kvh in kvs]
        sk = [sink_col(kvh, blk) for kvh in kvs]
        m = [jnp.maximum(jnp.maximum(jnp.max(sb[kvh], axis=-1, keepdims=True),
                                     jnp.max(sx[kvh], axis=-1, keepdims=True)), sk[kvh]) for kvh in kvs]
        eb = [jnp.exp(sb[kvh] - m[kvh]) for kvh in kvs]
        ex = [jnp.exp(sx[kvh] - m[kvh]) for kvh in kvs]
        den = [(jnp.exp(sk[kvh] - m[kvh]) + jnp.sum(eb[kvh], axis=-1, keepdims=True)
                + jnp.sum(ex[kvh], axis=-1, keepdims=True)) for kvh in kvs]
        pv = [(_dot(eb[kvh], vb[:, hs[kvh]])
               + _dot(ex[kvh], kv_ref[0:lc, kvw + kvh * HEAD_DIM:kvw + (kvh + 1) * HEAD_DIM]))
              for kvh in kvs]
        o_ref[pl.ds(qrow, blk), :] = unstack_heads([pv[kvh] / den[kvh] for kvh in kvs], blk)
        return carry

    lax.fori_loop(0, s_len // blk, block, 0)


def _swa(z, sink, cos_t, sin_t, layer, lc, gw, q_blk, kv_blk):
    b, t, _ = z.shape
    s_len = t - lc
    kvw = SWA_KV_HEADS * HEAD_DIM
    return pl.pallas_call(
        functools.partial(_swa_kernel, lc=lc, t=t, layer=layer),
        grid=(b,),
        in_specs=[pl.BlockSpec(memory_space=pltpu.SMEM),
                  pl.BlockSpec((None, t, gw), lambda bi: (bi, 0, q_blk)),
                  pl.BlockSpec((None, t, 2 * kvw), lambda bi: (bi, 0, kv_blk)),
                  pl.BlockSpec((s_len, gw), lambda bi: (0, 0)),
                  pl.BlockSpec((s_len, gw), lambda bi: (0, 0))],
        out_specs=pl.BlockSpec((None, t, gw), lambda bi: (bi, 0, 0)),
        out_shape=jax.ShapeDtypeStruct((b, t, gw), F32),
        scratch_shapes=[pltpu.VMEM((t, gw), F32), pltpu.VMEM((t, kvw), F32)],
        compiler_params=_params("arbitrary"),
        name="swa",
    )(sink, z, z, cos_t, sin_t)


def _gmlp_kernel(u_ref, v_ref, ws_ref, bs_ref, nw_ref, o_ref, *, t):
    gw = u_ref.shape[-1]
    ng = ws_ref.shape[0]
    cw = gw // ng
    ck = GMLP_CHUNK

    def chunk(ci, carry):
        r0 = pl.multiple_of(ci * ck, ck)
        u = jax.nn.gelu(u_ref[pl.ds(r0, ck), :])
        v = _rms(jax.nn.gelu(v_ref[pl.ds(r0, ck), :]), gw) * nw_ref[...]
        parts = [_dot(ws_ref[g], v[:, g * cw:(g + 1) * cw]) + bs_ref[:, g:g + 1] for g in range(ng)]
        o_ref[pl.ds(r0, ck), :] = u * jnp.concatenate(parts, axis=1)
        return carry

    lax.fori_loop(0, t // ck, chunk, 0)


def _gmlp(z, w_s, b_s_t, norm_w, layer, gw, u_blk):
    b, t, _ = z.shape
    ng = w_s.shape[1]
    return pl.pallas_call(
        functools.partial(_gmlp_kernel, t=t),
        grid=(b,),
        in_specs=[pl.BlockSpec((None, t, gw), lambda bi: (bi, 0, u_blk)),
                  pl.BlockSpec((None, t, gw), lambda bi: (bi, 0, u_blk + 1)),
                  pl.BlockSpec((None, ng, GMLP_CHUNK, GMLP_CHUNK), lambda bi: (layer, 0, 0, 0)),
                  pl.BlockSpec((None, GMLP_CHUNK, ng), lambda bi: (layer, 0, 0)),
                  pl.BlockSpec((None, 1, gw), lambda bi: (layer, 0, 0))],
        out_specs=pl.BlockSpec((None, t, gw), lambda bi: (bi, 0, 0)),
        out_shape=jax.ShapeDtypeStruct((b, t, gw), F32),
        compiler_params=_params("arbitrary"),
        name="gmlp",
    )(z, z, w_s, b_s_t, norm_w)


def _outmlp_kernel(x_ref, a_ref, b_ref, c_ref, d_ref, mod_ref, nw_ref, fw_ref, wo_ref, w1_ref, w2_ref,
                   o_ref, *, ff_blk, final):
    gw = a_ref.shape[-1]
    acc = None
    for g, ref in enumerate((a_ref, b_ref, c_ref, d_ref)):
        term = _dot(ref[...], wo_ref[g * gw:(g + 1) * gw, :])
        acc = term if acc is None else acc + term
    x1 = x_ref[...] + mod_ref[2:3, :] * acc
    h = _rms(x1, x1.shape[-1]) * nw_ref[...]
    h = (h * (1.0 + mod_ref[4:5, :]) + mod_ref[3:4, :]).astype(BF16)
    y = None
    for j in range(w1_ref.shape[-1] // ff_blk):
        hid = jnp.square(jnp.maximum(_dot(h, w1_ref[:, j * ff_blk:(j + 1) * ff_blk]), 0.0))
        term = _dot(hid, w2_ref[j * ff_blk:(j + 1) * ff_blk, :])
        y = term if y is None else y + term
    x2 = x1 + mod_ref[5:6, :] * y
    o_ref[...] = _rms(x2, x2.shape[-1]) * fw_ref[...] if final else x2


def _outmlp(xs, mixes, mod, norm_w, final_w, w_out, w1, w2, layer, lc, tm, final):
    b, t, d = xs.shape
    gw = mixes[0].shape[-1]
    dff = w1.shape[-1]
    nct = lc // tm
    off = nct if final else 0
    rows = t - off * tm
    tile = lambda w: pl.BlockSpec((None, tm, w), lambda bi, i: (bi, i + off, 0))
    const = lambda r, c: pl.BlockSpec((None, r, c), lambda bi, i: (layer, 0, 0),
                                      pipeline_mode=pl.Buffered(1))
    return pl.pallas_call(
        functools.partial(_outmlp_kernel, ff_blk=min(dff, 1024), final=final),
        grid=(b, rows // tm),
        in_specs=[tile(d), tile(gw), tile(gw), tile(gw), tile(gw),
                  pl.BlockSpec((None, None, None, N_MOD, d),
                               lambda bi, i: (layer, bi, jnp.where(i + off >= nct, 1, 0), 0, 0)),
                  pl.BlockSpec((None, 1, d), lambda bi, i: (layer, 0, 0)),
                  pl.BlockSpec((1, d), lambda bi, i: (0, 0)),
                  const(d, d), const(d, dff), const(dff, d)],
        out_specs=pl.BlockSpec((None, tm, d), lambda bi, i: (bi, i, 0)),
        out_shape=jax.ShapeDtypeStruct((b, rows, d), F32),
        compiler_params=_params("arbitrary", "arbitrary"),
        name="outmlp",
    )(xs, *mixes, mod, norm_w, final_w, w_out, w1, w2)


def _rope_tables(s_len, n_heads):
    rows = s_len // GRID_W
    row = jnp.repeat(jnp.arange(rows), GRID_W).astype(F32)
    col = (jnp.arange(rows * GRID_W) % GRID_W).astype(F32)
    n_freq = HEAD_DIM // 4
    inv = jnp.power(ROPE_THETA, -jnp.arange(n_freq, dtype=F32) / n_freq)
    ang = jnp.concatenate([row[:, None] * inv, col[:, None] * inv], axis=-1)
    cos, sin = jnp.cos(ang), jnp.sin(ang)
    return (jnp.tile(jnp.concatenate([cos, cos], axis=-1), (1, n_heads)),
            jnp.tile(jnp.concatenate([-sin, sin], axis=-1), (1, n_heads)))


def _lane_vec(parts, depth):
    out = jnp.zeros((depth, 1, GATE_W), F32)
    for off, val in parts:
        val = val.reshape(depth, 1, -1).astype(F32)
        out = lax.dynamic_update_slice(out, val, (0, 0, off))
    return out


def kernel(x, c, ctx, c_ctx, ada_w, ada_b, norm1_w, norm2_w, w_in, w_out, gdn_conv_w, gdn_a_log,
           gdn_dt_bias, gdn_norm_w, swa_sink, gmlp_w_s, gmlp_b_s, gmlp_norm_w, mlstm_ig_bias,
           mlstm_fg_bias, mlstm_norm_w, mlp_w1, mlp_w2, final_norm_w):
    bsz, s_len, d = x.shape
    lc = ctx.shape[1]
    depth = ada_w.shape[0]
    gw = d // 4
    nh = gw // HEAD_DIM
    ng = N_DIR * nh
    kvw = SWA_KV_HEADS * HEAD_DIM
    tm = 2 * LANES
    assert lc % tm == 0 and s_len % tm == 0 and s_len >= 3 * SWA_BLOCK and 2 * kvw == gw
    assert 4 * ng <= GATE_W

    sizes = (gw, gw, gw, gw, ng, ng, gw, kvw, kvw, gw, gw, gw, gw, gw, gw, ng, ng)
    offs = [0]
    for sz in sizes:
        offs.append(offs[-1] + sz)
    seg = lambda i: w_in[:, :, offs[i]:offs[i + 1]]
    gate_cols = jnp.concatenate([seg(4), seg(5), seg(15), seg(16)], axis=-1)
    gate_cols = jnp.pad(gate_cols, ((0, 0), (0, 0), (0, GATE_W - 4 * ng)))
    w_in_p = jnp.concatenate([seg(0), seg(1), seg(2), seg(3), seg(6), seg(7), seg(8), seg(9), seg(10),
                              seg(11), seg(12), seg(13), seg(14), gate_cols], axis=-1).astype(BF16)
    gate_blk = (w_in_p.shape[-1] - GATE_W) // GATE_W
    w_out_b, w1_b, w2_b = w_out.astype(BF16), mlp_w1.astype(BF16), mlp_w2.astype(BF16)

    alog_vec = _lane_vec([(0, gdn_a_log)], depth)
    gate_bias = _lane_vec([(0, gdn_dt_bias), (2 * ng, mlstm_ig_bias), (3 * ng, mlstm_fg_bias)], depth)
    cos_t, sin_t = _rope_tables(s_len, nh)
    b_s_t = jnp.swapaxes(gmlp_b_s, 1, 2)

    rows = -(-(bsz + 1) // SUBLANES) * SUBLANES
    cvec = jnp.concatenate([c, c_ctx[None, :], jnp.zeros((rows - bsz - 1, d), F32)], axis=0)
    mod_all = _adaln(cvec, ada_w, ada_b)
    mod_x = mod_all[:, :bsz].reshape(depth, bsz, 1, N_MOD, d)
    mod_c = jnp.broadcast_to(mod_all[:, bsz].reshape(depth, 1, 1, N_MOD, d), mod_x.shape)
    mod = jnp.concatenate([mod_c, mod_x], axis=2)

    n1 = norm1_w.reshape(depth, 1, d)
    n2 = norm2_w.reshape(depth, 1, d)
    gdn_nw = gdn_norm_w.reshape(depth, 1, HEAD_DIM)
    gmlp_nw = gmlp_norm_w.reshape(depth, 1, gw)
    mlstm_nw = mlstm_norm_w.reshape(depth, 1, gw)

    xs = jnp.concatenate([ctx, x], axis=1)
    for l in range(depth):
        z = _inproj(xs, mod, n1, w_in_p, l, lc, tm)
        mix_a = _gdn(z, alog_vec, gate_bias, gdn_conv_w, gdn_nw, l, lc, gw, gate_blk)
        mix_b = _swa(z, swa_sink, cos_t, sin_t, l, lc, gw, 4, 5)
        mix_c = _gmlp(z, gmlp_w_s, b_s_t, gmlp_nw, l, gw, 6)
        mix_d = _mlstm(z, gate_bias, mlstm_nw, l, lc, gw, 8, gate_blk)
        xs = _outmlp(xs, (mix_a, mix_b, mix_c, mix_d), mod, n2, final_norm_w.reshape(1, d),
                     w_out_b, w1_b, w2_b, l, lc, tm, final=l == depth - 1)
    return xs
```

```python
import functools

import jax
import jax.numpy as jnp
from jax import lax
from jax.experimental import pallas as pl
from jax.experimental.pallas import tpu as pltpu

F32 = jnp.float32
BF16 = jnp.bfloat16

HEAD_DIM = 64
N_DIR = 2
N_MOD = 6
NORM_EPS = 1e-6
SCAN_CHUNK = 64
GDN_CONV = 5
SWA_KV_HEADS = 2
SWA_WINDOW = 128
SWA_BLOCK = 128
GMLP_CHUNK = 128
GRID_W = 64
ROPE_THETA = 10000.0

LANES = 128
SUBLANES = 8
GATE_W = LANES
NEG_BIG = -1e30
VMEM_LIMIT = 56 * 1024 * 1024
SOLVE_PIECES = 3


def _dot(a, b):
    return jnp.dot(a.astype(BF16), b.astype(BF16), preferred_element_type=F32)


def _dot_nt(a, b):
    return lax.dot_general(a.astype(BF16), b.astype(BF16), (((1,), (1,)), ((), ())),
                           preferred_element_type=F32)


def _dot_tn(a, b):
    return lax.dot_general(a.astype(BF16), b.astype(BF16), (((0,), (0,)), ((), ())),
                           preferred_element_type=F32)


def _split_bf16(x, parts):
    out = []
    for _ in range(parts - 1):
        hi = x.astype(BF16)
        out.append(hi)
        x = x - hi.astype(F32)
    out.append(x.astype(BF16))
    return out


def _dot_mask(mask, x):
    acc = None
    for piece in _split_bf16(x, 3):
        term = jnp.dot(mask, piece, preferred_element_type=F32)
        acc = term if acc is None else acc + term
    return acc


def _dot_mask_r(x, mask):
    acc = None
    for piece in _split_bf16(x, 3):
        term = jnp.dot(piece, mask, preferred_element_type=F32)
        acc = term if acc is None else acc + term
    return acc


_NN = (((1,), (0,)), ((), ()))
_NT = (((1,), (1,)), ((), ()))
_TN = (((0,), (0,)), ((), ()))


def _dot_pieces(a, b, dims=_NN):
    dg = lambda x, y: lax.dot_general(x, y, dims, preferred_element_type=F32)
    out = dg(a[0], b[0]) + (dg(a[0], b[1]) + dg(a[1], b[0]))
    if len(a) == 3:
        out = out + ((dg(a[0], b[2]) + dg(a[2], b[0])) + dg(a[1], b[1]))
    return out


def _dot_solve(a, b, dims=_NN):
    return _dot_pieces(_split_bf16(a, 2), _split_bf16(b, 2), dims)


def _params(*sem):
    return pltpu.CompilerParams(dimension_semantics=sem, vmem_limit_bytes=VMEM_LIMIT)


def _rms(x, n):
    return x * lax.rsqrt(jnp.sum(x * x, axis=-1, keepdims=True) * (1.0 / n) + NORM_EPS)


def _tri_masks(c):
    row = lax.broadcasted_iota(jnp.int32, (c, c), 0)
    col = lax.broadcasted_iota(jnp.int32, (c, c), 1)
    incl = (row >= col, row <= col)
    strict = (row > col, row < col)
    return incl, strict


def _scan_batch(b):
    return 2 if b % 2 == 0 else 1


def _scan_chunk_index(i, d, ncc, nc):
    if d == 0:
        return i
    return jnp.where(i < ncc, ncc - 1 - i, ncc + nc - 1 - i)


def _adaln_kernel(c_ref, w_ref, b_ref, o_ref):
    o_ref[...] = _dot(jax.nn.silu(c_ref[...]), w_ref[...]) + b_ref[...]


def _adaln(cvec, ada_w, ada_b):
    depth, d, n = ada_w.shape
    rows = cvec.shape[0]
    bn = n // N_MOD
    return pl.pallas_call(
        _adaln_kernel,
        grid=(depth, n // bn),
        in_specs=[pl.BlockSpec((rows, d), lambda l, j: (0, 0)),
                  pl.BlockSpec((None, d, bn), lambda l, j: (l, 0, j)),
                  pl.BlockSpec((None, 1, bn), lambda l, j: (l, 0, j))],
        out_specs=pl.BlockSpec((None, rows, bn), lambda l, j: (l, 0, j)),
        out_shape=jax.ShapeDtypeStruct((depth, rows, n), F32),
        compiler_params=_params("arbitrary", "arbitrary"),
        name="adaln",
    )(cvec, ada_w, ada_b.reshape(depth, 1, n))


def _inproj_kernel(x_ref, mod_ref, nw_ref, w_ref, z_ref):
    x = x_ref[...]
    h = _rms(x, x.shape[-1]) * nw_ref[...]
    h = h * (1.0 + mod_ref[1:2, :]) + mod_ref[0:1, :]
    z_ref[...] = _dot(h, w_ref[...])


def _inproj(xs, mod, norm_w, w_in, layer, lc, tm):
    b, t, d = xs.shape
    n = w_in.shape[-1]
    nct = lc // tm
    return pl.pallas_call(
        _inproj_kernel,
        grid=(b, t // tm),
        in_specs=[pl.BlockSpec((None, tm, d), lambda bi, i: (bi, i, 0)),
                  pl.BlockSpec((None, None, None, N_MOD, d),
                               lambda bi, i: (layer, bi, jnp.where(i >= nct, 1, 0), 0, 0)),
                  pl.BlockSpec((None, 1, d), lambda bi, i: (layer, 0, 0)),
                  pl.BlockSpec((None, d, n), lambda bi, i: (layer, 0, 0))],
        out_specs=pl.BlockSpec((None, tm, n), lambda bi, i: (bi, i, 0)),
        out_shape=jax.ShapeDtypeStruct((b, t, n), F32),
        compiler_params=_params("arbitrary", "arbitrary"),
        name="inproj",
    )(xs, mod, norm_w, w_in)


def _gdn_kernel(alog_ref, gb_ref, q_ref, k_ref, v_ref, z_ref, g_ref, cw_ref, nw_ref, o_ref,
                of_ref, ob_ref, s_ref, *, lc, t):
    c = SCAN_CHUNK
    nb = q_ref.shape[0]
    gw = q_ref.shape[-1]
    nh = gw // HEAD_DIM
    nc, ncc = t // c, lc // c
    assert c >= 4 and c & (c - 1) == 0
    incl, strict = _tri_masks(c)
    inclb = [jnp.where(m, 1.0, 0.0).astype(BF16) for m in incl]
    eye = jnp.where(incl[0] & incl[1], 1.0, 0.0)
    neg_a = -jnp.exp(alog_ref[...])
    head_ones = jnp.where(lax.broadcasted_iota(jnp.int32, (gw, gw), 0) // HEAD_DIM
                          == lax.broadcasted_iota(jnp.int32, (gw, gw), 1) // HEAD_DIM,
                          1.0, 0.0).astype(BF16)
    s_ref[...] = jnp.zeros_like(s_ref)

    def prep(bb, ci):
        r0 = pl.multiple_of(ci * c, c)
        lo = pl.multiple_of(jnp.maximum(r0 - SUBLANES, 0), SUBLANES)
        hi = pl.multiple_of(jnp.minimum(r0 + c, t - SUBLANES), SUBLANES)
        tpos = r0 + lax.broadcasted_iota(jnp.int32, (c, 1), 0)
        seg_lo = jnp.where(tpos < lc, 0, lc)
        seg_hi = jnp.where(tpos < lc, lc, t)
        half = GDN_CONV // 2
        outs = []
        for idx, ref in enumerate((q_ref, k_ref, v_ref)):
            x = jnp.concatenate([ref[bb, pl.ds(lo, SUBLANES), :], ref[bb, pl.ds(r0, c), :],
                                 ref[bb, pl.ds(hi, SUBLANES), :]], axis=0)
            acc = None
            for tap in range(GDN_CONV):
                dlt = tap - half
                xs = x[SUBLANES + dlt:SUBLANES + dlt + c]
                if dlt != 0:
                    ok = jnp.where((tpos + dlt >= seg_lo) & (tpos + dlt < seg_hi), 1.0, 0.0)
                    xs = jnp.where(ok > 0.5, xs, 0.0)
                term = xs * cw_ref[tap:tap + 1, idx * gw:(idx + 1) * gw]
                acc = term if acc is None else acc + term
            outs.append(jax.nn.silu(acc))
        qn = outs[0] * (lax.rsqrt(_dot_mask_r(outs[0] * outs[0], head_ones) + NORM_EPS)
                        * HEAD_DIM ** -0.5)
        kn = outs[1] * lax.rsqrt(_dot_mask_r(outs[1] * outs[1], head_ones) + NORM_EPS)
        heads = [slice(h * HEAD_DIM, (h + 1) * HEAD_DIM) for h in range(nh)]
        return (r0, [qn[:, sl] for sl in heads], [kn[:, sl] for sl in heads],
                [outs[2][:, sl] for sl in heads], g_ref[bb, pl.ds(r0, c), :])

    def step(i, carry):
        chains = []
        rows = []
        for bb, d in [(bb, d) for bb in range(nb) for d in range(N_DIR)]:
            r0, qs, ks, vs, gt = prep(bb, _scan_chunk_index(i, d, ncc, nc))
            rows.append(r0)
            gval = neg_a * jax.nn.softplus(gt + gb_ref[...])
            beta = jax.nn.sigmoid(gt)
            gcum = _dot_mask(inclb[d], gval)
            gtot = gcum[c - 1:c, :] if d == 0 else gcum[0:1, :]
            gcum_t = gcum.T
            egc = jnp.exp(gcum)
            ekd = jnp.exp(gtot - gcum)
            egl = jnp.exp(gtot)
            for h in range(nh):
                col = d * nh + h
                bcol = N_DIR * nh + col
                kb = ks[h] * beta[:, bcol:bcol + 1]
                chains.append(dict(
                    bb=bb, d=d, slot=bb * N_DIR * nh + col, q=qs[h], k=ks[h], kb=kb,
                    decay=jnp.exp(jnp.where(incl[d], gcum[:, col:col + 1] - gcum_t[col:col + 1, :],
                                            NEG_BIG)),
                    rhs=jnp.concatenate([vs[h] * beta[:, bcol:bcol + 1], kb * egc[:, col:col + 1]],
                                        axis=1),
                    q_dec=qs[h] * egc[:, col:col + 1], k_dec=ks[h] * ekd[:, col:col + 1],
                    g_last=egl[:, col:col + 1]))
        for ch in chains:
            ch["both"] = _dot_nt(jnp.concatenate([ch["kb"], ch["q"]], axis=0), ch["k"])
        for ch in chains:
            ch["attn"] = ch["both"][c:] * ch["decay"]
            ch["n"] = jnp.where(strict[ch["d"]], -(ch["both"][:c] * ch["decay"]), 0.0)
            ch["p"] = eye + ch["n"]
        for ch in chains:
            ns = _split_bf16(ch["n"], SOLVE_PIECES)
            ch["n"] = _dot_pieces(ns, ns)
        span = 4
        while span < c:
            for ch in chains:
                ns, ps = _split_bf16(ch["n"], SOLVE_PIECES), _split_bf16(ch["p"], SOLVE_PIECES)
                stacked = [jnp.concatenate([ns[i], ps[i]], axis=0) for i in range(SOLVE_PIECES)]
                nxt = _dot_pieces(stacked, ns)
                ch["p"] = ch["p"] + nxt[c:]
                ch["n"] = nxt[:c]
            span *= 2
        for ch in chains:
            ch["p"] = ch["p"] + _dot_pieces(_split_bf16(ch["p"], SOLVE_PIECES),
                                            _split_bf16(ch["n"], SOLVE_PIECES))
        for ch in chains:
            ch["sol"] = _dot_pieces(_split_bf16(ch["p"], SOLVE_PIECES),
                                    _split_bf16(ch["rhs"], SOLVE_PIECES))
        for ch in chains:
            ch["state"] = s_ref[ch["slot"]]
            ch["ws"] = _dot(jnp.concatenate([ch["sol"][:, HEAD_DIM:], ch["q_dec"]], axis=0), ch["state"])
        for ch in chains:
            ch["v_new"] = ch["sol"][:, :HEAD_DIM] - ch["ws"][:c]
            ch["o"] = ch["ws"][c:] + _dot(ch["attn"], ch["v_new"])
        for ch in chains:
            s_ref[ch["slot"]] = ch["state"] * ch["g_last"] + _dot_tn(ch["k_dec"], ch["v_new"])
        for bb in range(nb):
            for d, ref in enumerate((of_ref, ob_ref)):
                ref[bb, pl.ds(rows[bb * N_DIR + d], c), :] = jnp.concatenate(
                    [ch["o"] for ch in chains if ch["d"] == d and ch["bb"] == bb], axis=1)
        return carry

    lax.fori_loop(0, nc, step, 0)

    rb = 2 * LANES

    def finish(r, carry):
        r0 = pl.multiple_of(r * rb, rb)
        for bb in range(nb):
            o = of_ref[bb, pl.ds(r0, rb), :] + ob_ref[bb, pl.ds(r0, rb), :]
            parts = []
            for h in range(nh):
                parts.append(_rms(o[:, h * HEAD_DIM:(h + 1) * HEAD_DIM], HEAD_DIM) * nw_ref[...])
            o_ref[bb, pl.ds(r0, rb), :] = (jnp.concatenate(parts, axis=1)
                                           * jax.nn.silu(z_ref[bb, pl.ds(r0, rb), :]))
        return carry

    lax.fori_loop(0, t // rb, finish, 0)


def _gdn(z, alog_vec, gate_bias, conv_w, norm_w, layer, lc, gw, gate_blk):
    b, t, _ = z.shape
    nh = gw // HEAD_DIM
    nb = _scan_batch(b)
    col = lambda j: pl.BlockSpec((nb, t, gw), lambda bi: (bi, 0, j), pipeline_mode=pl.Buffered(1))
    return pl.pallas_call(
        functools.partial(_gdn_kernel, lc=lc, t=t),
        grid=(b // nb,),
        in_specs=[pl.BlockSpec((None, 1, GATE_W), lambda bi: (layer, 0, 0)),
                  pl.BlockSpec((None, 1, GATE_W), lambda bi: (layer, 0, 0)),
                  col(0), col(1), col(2), col(3),
                  pl.BlockSpec((nb, t, GATE_W), lambda bi: (bi, 0, gate_blk),
                               pipeline_mode=pl.Buffered(1)),
                  pl.BlockSpec((None, GDN_CONV, 3 * gw), lambda bi: (layer, 0, 0)),
                  pl.BlockSpec((None, 1, HEAD_DIM), lambda bi: (layer, 0, 0))],
        out_specs=pl.BlockSpec((nb, t, gw), lambda bi: (bi, 0, 0)),
        out_shape=jax.ShapeDtypeStruct((b, t, gw), F32),
        scratch_shapes=[pltpu.VMEM((nb, t, gw), F32), pltpu.VMEM((nb, t, gw), F32),
                        pltpu.VMEM((nb * N_DIR * nh, HEAD_DIM, HEAD_DIM), F32)],
        compiler_params=_params("arbitrary"),
        name="gdn",
    )(alog_vec, gate_bias, z, z, z, z, z, conv_w, norm_w)


def _mlstm_kernel(gb_ref, q_ref, k_ref, v_ref, og_ref, g_ref, nw_ref, o_ref,
                  hf_ref, hb_ref, cn_ref, m_ref, *, lc, t):
    c = SCAN_CHUNK
    nb = q_ref.shape[0]
    gw = q_ref.shape[-1]
    nh = gw // HEAD_DIM
    nc, ncc = t // c, lc // c
    assert c == HEAD_DIM
    icol = 2 * N_DIR * nh
    fcol = 3 * N_DIR * nh

    ii = lax.broadcasted_iota(jnp.int32, (c, gw), 0)
    jj = lax.broadcasted_iota(jnp.int32, (c, gw), 1)
    tri = (ii >= jj % HEAD_DIM, ii <= jj % HEAD_DIM)
    eye4 = jnp.where(ii == jj % HEAD_DIM, 1.0, 0.0)
    ri = lax.broadcasted_iota(jnp.int32, (c, c), 0)
    ci = lax.broadcasted_iota(jnp.int32, (c, c), 1)
    inclb = [jnp.where(ri >= ci, 1.0, 0.0).astype(BF16), jnp.where(ri <= ci, 1.0, 0.0).astype(BF16)]
    ones_cc = jnp.ones((c, c), BF16)
    bi_ = lax.broadcasted_iota(jnp.int32, (gw, gw), 0) // HEAD_DIM
    bj_ = lax.broadcasted_iota(jnp.int32, (gw, gw), 1) // HEAD_DIM
    block = bi_ == bj_
    block_ones = jnp.where(block, 1.0, 0.0).astype(BF16)
    block2 = jnp.concatenate([block, block], axis=1)
    gl = lax.broadcasted_iota(jnp.int32, (GATE_W, gw), 0)
    gh = lax.broadcasted_iota(jnp.int32, (GATE_W, gw), 1) // HEAD_DIM
    sel_f = [jnp.where(gl == fcol + d * nh + gh, 1.0, 0.0).astype(BF16) for d in range(N_DIR)]
    sel_i = [jnp.where(gl == icol + d * nh + gh, 1.0, 0.0).astype(BF16) for d in range(N_DIR)]
    ones_v = jnp.ones((c, gw), F32)

    cn_ref[...] = jnp.zeros_like(cn_ref)
    m_ref[...] = jnp.zeros_like(m_ref)

    def step(i, carry):
        units = [(bb, d) for bb in range(nb) for d in range(N_DIR)]
        es = range(len(units))
        r0 = [pl.multiple_of(_scan_chunk_index(i, d, ncc, nc) * c, c) for _, d in units]
        gt = [g_ref[bb, pl.ds(r0[e], c), :] + gb_ref[...] for e, (bb, _) in enumerate(units)]
        q_all = [q_ref[bb, pl.ds(r0[e], c), :] * HEAD_DIM ** -0.5 for e, (bb, _) in enumerate(units)]
        k_all = [k_ref[bb, pl.ds(r0[e], c), :] for e, (bb, _) in enumerate(units)]
        v_all = [v_ref[bb, pl.ds(r0[e], c), :] for e, (bb, _) in enumerate(units)]
        k_bd = [jnp.where(block, jnp.concatenate([k_all[e]] * nh, axis=0), 0.0) for e in es]
        qk = [_dot_nt(q_all[e], k_bd[e]) for e in es]
        cn = [cn_ref[e] for e in es]
        qcn = [_dot(q_all[e], cn[e]) for e in es]
        bcum = [_dot_mask(inclb[d], jax.nn.log_sigmoid(gt[e])) for e, (_, d) in enumerate(units)]
        b_col = [_dot_mask_r(bcum[e], sel_f[d]) for e, (_, d) in enumerate(units)]
        ig_col = [_dot_mask_r(gt[e], sel_i[d]) for e, (_, d) in enumerate(units)]
        x_col = [ig_col[e] - b_col[e] for e in es]
        x_row = [_dot_mask(ones_cc, x_col[e] * eye4) for e in es]
        b_row = [_dot_mask(ones_cc, b_col[e] * eye4) for e in es]
        m_old = [m_ref[e] for e in es]
        b_last = [b_col[e][c - 1:c, :] if d == 0 else b_col[e][0:1, :] for e, (_, d) in enumerate(units)]
        wst = [b_last[e] - b_col[e] + ig_col[e] for e in es]
        m_new = [jnp.maximum(b_last[e] + m_old[e], jnp.max(wst[e], axis=0, keepdims=True)) for e in es]
        k_w = [k_all[e] * jnp.exp(wst[e] - m_new[e]) for e in es]
        upd = [_dot_tn(k_w[e], jnp.concatenate([v_all[e], ones_v], axis=1)) for e in es]
        for e in es:
            c_decay = jnp.exp(b_last[e] + m_old[e] - m_new[e])
            cn_ref[e] = (jnp.concatenate([c_decay, c_decay], axis=1) * cn[e]
                         + jnp.where(block2, upd[e], 0.0))
            m_ref[e] = m_new[e]
        dmat_t = [jnp.where(tri[1 - d], b_row[e] + x_col[e], NEG_BIG) for e, (_, d) in enumerate(units)]
        mt_row = [jnp.maximum(b_row[e][0:1, :] + m_old[e], jnp.max(dmat_t[e], axis=0, keepdims=True))
                  for e in es]
        m_t = [_dot_mask_r(mt_row[e] * eye4, block_ones) for e in es]
        s = [qk[e] * jnp.exp(jnp.where(tri[d], b_col[e] + x_row[e], NEG_BIG) - m_t[e])
             for e, (_, d) in enumerate(units)]
        v_bd = [jnp.where(block, jnp.concatenate([v_all[e]] * nh, axis=0), 0.0) for e in es]
        sv = [_dot(s[e], jnp.concatenate([v_bd[e], block_ones.astype(F32)], axis=1)) for e in es]
        for e, (bb, d) in enumerate(units):
            ref = hf_ref if d == 0 else hb_ref
            w_inter = jnp.exp(b_col[e] + m_old[e] - m_t[e])
            tot = jnp.concatenate([w_inter, w_inter], axis=1) * qcn[e] + sv[e]
            ref[bb, pl.ds(r0[e], c), :] = (tot[:, :gw]
                                           / jnp.maximum(jnp.abs(tot[:, gw:]), jnp.exp(-m_t[e])))
        return carry

    lax.fori_loop(0, nc, step, 0)

    rb = 2 * LANES

    def finish(r, carry):
        r0 = pl.multiple_of(r * rb, rb)
        for bb in range(nb):
            hsum = hf_ref[bb, pl.ds(r0, rb), :] + hb_ref[bb, pl.ds(r0, rb), :]
            parts = [_rms(hsum[:, h * HEAD_DIM:(h + 1) * HEAD_DIM], HEAD_DIM) for h in range(nh)]
            o_ref[bb, pl.ds(r0, rb), :] = (jax.nn.sigmoid(og_ref[bb, pl.ds(r0, rb), :])
                                           * (jnp.concatenate(parts, axis=1) * nw_ref[...]))
        return carry

    lax.fori_loop(0, t // rb, finish, 0)


def _mlstm(z, gate_bias, norm_w, layer, lc, gw, first_blk, gate_blk):
    b, t, _ = z.shape
    nb = _scan_batch(b)
    col = lambda j: pl.BlockSpec((nb, t, gw), lambda bi: (bi, 0, first_blk + j),
                                 pipeline_mode=pl.Buffered(1))
    return pl.pallas_call(
        functools.partial(_mlstm_kernel, lc=lc, t=t),
        grid=(b // nb,),
        in_specs=[pl.BlockSpec((None, 1, GATE_W), lambda bi: (layer, 0, 0)),
                  col(0), col(1), col(2), col(3),
                  pl.BlockSpec((nb, t, GATE_W), lambda bi: (bi, 0, gate_blk),
                               pipeline_mode=pl.Buffered(1)),
                  pl.BlockSpec((None, 1, gw), lambda bi: (layer, 0, 0))],
        out_specs=pl.BlockSpec((nb, t, gw), lambda bi: (bi, 0, 0)),
        out_shape=jax.ShapeDtypeStruct((b, t, gw), F32),
        scratch_shapes=[pltpu.VMEM((nb, t, gw), F32), pltpu.VMEM((nb, t, gw), F32),
                        pltpu.VMEM((nb * N_DIR, gw, 2 * gw), F32),
                        pltpu.VMEM((nb * N_DIR, 1, gw), F32)],
        compiler_params=_params("arbitrary"),
        name="mlstm",
    )(gate_bias, z, z, z, z, z, norm_w)


def _swa_kernel(sink_ref, q_ref, kv_ref, cos_ref, sin_ref, o_ref, qr_ref, kr_ref, *, lc, t, layer):
    gw = q_ref.shape[-1]
    kvw = SWA_KV_HEADS * HEAD_DIM
    grp = gw // kvw
    s_len = t - lc
    blk = SWA_BLOCK
    band = 3 * blk
    scale = HEAD_DIM ** -0.5
    half = HEAD_DIM // 2

    qr_ref[0:lc, :] = q_ref[0:lc, :] * scale
    kr_ref[0:lc, :] = kv_ref[0:lc, 0:kvw]

    lane = lax.broadcasted_iota(jnp.int32, (1, gw), 1)
    first_half = (lane % HEAD_DIM) < half

    def rope(x, cs, sn):
        w = x.shape[-1]
        rot = jnp.where(first_half[:, :w], pltpu.roll(x, w - half, 1), pltpu.roll(x, half, 1))
        return x * cs + rot * sn

    rb = 2 * LANES

    def rope_rows(r, carry):
        p0 = pl.multiple_of(r * rb, rb)
        r0 = pl.multiple_of(lc + r * rb, rb)
        cs, sn = cos_ref[pl.ds(p0, rb), :], sin_ref[pl.ds(p0, rb), :]
        qr_ref[pl.ds(r0, rb), :] = rope(q_ref[pl.ds(r0, rb), :], cs, sn) * scale
        kr_ref[pl.ds(r0, rb), :] = rope(kv_ref[pl.ds(r0, rb), 0:kvw], cs[:, :kvw], sn[:, :kvw])
        return carry

    lax.fori_loop(0, s_len // rb, rope_rows, 0)

    def sink_col(kvh, rows):
        ridx = lax.broadcasted_iota(jnp.int32, (grp * rows, 1), 0)
        out = jnp.full((grp * rows, 1), sink_ref[layer, kvh * grp], F32)
        for g in range(1, grp):
            out = jnp.where(ridx >= g * rows, sink_ref[layer, kvh * grp + g], out)
        return out

    def stack_heads(qb, kvh):
        return jnp.concatenate([qb[:, (kvh * grp + g) * HEAD_DIM:(kvh * grp + g + 1) * HEAD_DIM]
                                for g in range(grp)], axis=0)

    def unstack_heads(o_list, rows):
        return jnp.concatenate([o[g * rows:(g + 1) * rows] for o in o_list for g in range(grp)], axis=1)

    outs = []
    for kvh in range(SWA_KV_HEADS):
        hs = slice(kvh * HEAD_DIM, (kvh + 1) * HEAD_DIM)
        q2 = stack_heads(qr_ref[0:lc, :], kvh)
        s = _dot_nt(q2, kr_ref[0:lc, hs])
        sk = sink_col(kvh, lc)
        m = jnp.maximum(jnp.max(s, axis=-1, keepdims=True), sk)
        e = jnp.exp(s - m)
        den = jnp.exp(sk - m) + jnp.sum(e, axis=-1, keepdims=True)
        outs.append(_dot(e, kv_ref[0:lc, kvw + kvh * HEAD_DIM:kvw + (kvh + 1) * HEAD_DIM]) / den)
    o_ref[0:lc, :] = unstack_heads(outs, lc)

    def block(n, carry):
        qrow = pl.multiple_of(lc + n * blk, blk)
        boff = pl.multiple_of(jnp.clip((n - 1) * blk, 0, s_len - band), blk)
        start = pl.multiple_of(lc + boff, blk)
        qb = qr_ref[pl.ds(qrow, blk), :]
        kb = kr_ref[pl.ds(start, band), :]
        vb = kv_ref[pl.ds(start, band), kvw:2 * kvw]
        qpos = n * blk + lax.broadcasted_iota(jnp.int32, (grp * blk, band), 0) % blk
        kpos = boff + lax.broadcasted_iota(jnp.int32, (grp * blk, band), 1)
        mask = jnp.abs(qpos - kpos) <= SWA_WINDOW
        kvs = range(SWA_KV_HEADS)
        hs = [slice(kvh * HEAD_DIM, (kvh + 1) * HEAD_DIM) for kvh in kvs]
        q2 = [stack_heads(qb, kvh) for kvh in kvs]
        sb = [jnp.where(mask, _dot_nt(q2[kvh], kb[:, hs[kvh]]), NEG_BIG) for kvh in kvs]
        sx = [_dot_nt(q2[kvh], kr_ref[0:lc, hs[kvh]]) for kvh in kvs]
        sk = [sink_col(kvh, blk) for kvh in kvs]
        m = [jnp.maximum(jnp.maximum(jnp.max(sb[kvh], axis=-1, keepdims=True),
                                     jnp.max(sx[kvh], axis=-1, keepdims=True)), sk[kvh]) for kvh in kvs]
        eb = [jnp.exp(sb[kvh] - m[kvh]) for kvh in kvs]
        ex = [jnp.exp(sx[kvh] - m[kvh]) for kvh in kvs]
        den = [(jnp.exp(sk[kvh] - m[kvh]) + jnp.sum(eb[kvh], axis=-1, keepdims=True)
                + jnp.sum(ex[kvh], axis=-1, keepdims=True)) for kvh in kvs]
        pv = [(_dot(eb[kvh], vb[:, hs[kvh]])
               + _dot(ex[kvh], kv_ref[0:lc, kvw + kvh * HEAD_DIM:kvw + (kvh + 1) * HEAD_DIM]))
              for kvh in kvs]
        o_ref[pl.ds(qrow, blk), :] = unstack_heads([pv[kvh] / den[kvh] for kvh in kvs], blk)
        return carry

    lax.fori_loop(0, s_len // blk, block, 0)


def _swa(z, sink, cos_t, sin_t, layer, lc, gw, q_blk, kv_blk):
    b, t, _ = z.shape
    s_len = t - lc
    kvw = SWA_KV_HEADS * HEAD_DIM
    return pl.pallas_call(
        functools.partial(_swa_kernel, lc=lc, t=t, layer=layer),
        grid=(b,),
        in_specs=[pl.BlockSpec(memory_space=pltpu.SMEM),
                  pl.BlockSpec((None, t, gw), lambda bi: (bi, 0, q_blk)),
                  pl.BlockSpec((None, t, 2 * kvw), lambda bi: (bi, 0, kv_blk)),
                  pl.BlockSpec((s_len, gw), lambda bi: (0, 0)),
                  pl.BlockSpec((s_len, gw), lambda bi: (0, 0))],
        out_specs=pl.BlockSpec((None, t, gw), lambda bi: (bi, 0, 0)),
        out_shape=jax.ShapeDtypeStruct((b, t, gw), F32),
        scratch_shapes=[pltpu.VMEM((t, gw), F32), pltpu.VMEM((t, kvw), F32)],
        compiler_params=_params("arbitrary"),
        name="swa",
    )(sink, z, z, cos_t, sin_t)


def _gmlp_kernel(u_ref, v_ref, ws_ref, bs_ref, nw_ref, o_ref, *, t):
    gw = u_ref.shape[-1]
    ng = ws_ref.shape[0]
    cw = gw // ng
    ck = GMLP_CHUNK

    def chunk(ci, carry):
        r0 = pl.multiple_of(ci * ck, ck)
        u = jax.nn.gelu(u_ref[pl.ds(r0, ck), :])
        v = _rms(jax.nn.gelu(v_ref[pl.ds(r0, ck), :]), gw) * nw_ref[...]
        parts = [_dot(ws_ref[g], v[:, g * cw:(g + 1) * cw]) + bs_ref[:, g:g + 1] for g in range(ng)]
        o_ref[pl.ds(r0, ck), :] = u * jnp.concatenate(parts, axis=1)
        return carry

    lax.fori_loop(0, t // ck, chunk, 0)


def _gmlp(z, w_s, b_s_t, norm_w, layer, gw, u_blk):
    b, t, _ = z.shape
    ng = w_s.shape[1]
    return pl.pallas_call(
        functools.partial(_gmlp_kernel, t=t),
        grid=(b,),
        in_specs=[pl.BlockSpec((None, t, gw), lambda bi: (bi, 0, u_blk)),
                  pl.BlockSpec((None, t, gw), lambda bi: (bi, 0, u_blk + 1)),
                  pl.BlockSpec((None, ng, GMLP_CHUNK, GMLP_CHUNK), lambda bi: (layer, 0, 0, 0)),
                  pl.BlockSpec((None, GMLP_CHUNK, ng), lambda bi: (layer, 0, 0)),
                  pl.BlockSpec((None, 1, gw), lambda bi: (layer, 0, 0))],
        out_specs=pl.BlockSpec((None, t, gw), lambda bi: (bi, 0, 0)),
        out_shape=jax.ShapeDtypeStruct((b, t, gw), F32),
        compiler_params=_params("arbitrary"),
        name="gmlp",
    )(z, z, w_s, b_s_t, norm_w)


def _outmlp_kernel(x_ref, a_ref, b_ref, c_ref, d_ref, mod_ref, nw_ref, fw_ref, wo_ref, w1_ref, w2_ref,
                   o_ref, *, ff_blk, final):
    gw = a_ref.shape[-1]
    acc = None
    for g, ref in enumerate((a_ref, b_ref, c_ref, d_ref)):
        term = _dot(ref[...], wo_ref[g * gw:(g + 1) * gw, :])
        acc = term if acc is None else acc + term
    x1 = x_ref[...] + mod_ref[2:3, :] * acc
    h = _rms(x1, x1.shape[-1]) * nw_ref[...]
    h = (h * (1.0 + mod_ref[4:5, :]) + mod_ref[3:4, :]).astype(BF16)
    y = None
    for j in range(w1_ref.shape[-1] // ff_blk):
        hid = jnp.square(jnp.maximum(_dot(h, w1_ref[:, j * ff_blk:(j + 1) * ff_blk]), 0.0))
        term = _dot(hid, w2_ref[j * ff_blk:(j + 1) * ff_blk, :])
        y = term if y is None else y + term
    x2 = x1 + mod_ref[5:6, :] * y
    o_ref[...] = _rms(x2, x2.shape[-1]) * fw_ref[...] if final else x2


def _outmlp(xs, mixes, mod, norm_w, final_w, w_out, w1, w2, layer, lc, tm, final):
    b, t, d = xs.shape
    gw = mixes[0].shape[-1]
    dff = w1.shape[-1]
    nct = lc // tm
    off = nct if final else 0
    rows = t - off * tm
    tile = lambda w: pl.BlockSpec((None, tm, w), lambda bi, i: (bi, i + off, 0))
    const = lambda r, c: pl.BlockSpec((None, r, c), lambda bi, i: (layer, 0, 0),
                                      pipeline_mode=pl.Buffered(1))
    return pl.pallas_call(
        functools.partial(_outmlp_kernel, ff_blk=min(dff, 1024), final=final),
        grid=(b, rows // tm),
        in_specs=[tile(d), tile(gw), tile(gw), tile(gw), tile(gw),
                  pl.BlockSpec((None, None, None, N_MOD, d),
                               lambda bi, i: (layer, bi, jnp.where(i + off >= nct, 1, 0), 0, 0)),
                  pl.BlockSpec((None, 1, d), lambda bi, i: (layer, 0, 0)),
                  pl.BlockSpec((1, d), lambda bi, i: (0, 0)),
                  const(d, d), const(d, dff), const(dff, d)],
        out_specs=pl.BlockSpec((None, tm, d), lambda bi, i: (bi, i, 0)),
        out_shape=jax.ShapeDtypeStruct((b, rows, d), F32),
        compiler_params=_params("arbitrary", "arbitrary"),
        name="outmlp",
    )(xs, *mixes, mod, norm_w, final_w, w_out, w1, w2)


def _rope_tables(s_len, n_heads):
    rows = s_len // GRID_W
    row = jnp.repeat(jnp.arange(rows), GRID_W).astype(F32)
    col = (jnp.arange(rows * GRID_W) % GRID_W).astype(F32)
    n_freq = HEAD_DIM // 4
    inv = jnp.power(ROPE_THETA, -jnp.arange(n_freq, dtype=F32) / n_freq)
    ang = jnp.concatenate([row[:, None] * inv, col[:, None] * inv], axis=-1)
    cos, sin = jnp.cos(ang), jnp.sin(ang)
    return (jnp.tile(jnp.concatenate([cos, cos], axis=-1), (1, n_heads)),
            jnp.tile(jnp.concatenate([-sin, sin], axis=-1), (1, n_heads)))


def _lane_vec(parts, depth):
    out = jnp.zeros((depth, 1, GATE_W), F32)
    for off, val in parts:
        val = val.reshape(depth, 1, -1).astype(F32)
        out = lax.dynamic_update_slice(out, val, (0, 0, off))
    return out


def kernel(x, c, ctx, c_ctx, ada_w, ada_b, norm1_w, norm2_w, w_in, w_out, gdn_conv_w, gdn_a_log,
           gdn_dt_bias, gdn_norm_w, swa_sink, gmlp_w_s, gmlp_b_s, gmlp_norm_w, mlstm_ig_bias,
           mlstm_fg_bias, mlstm_norm_w, mlp_w1, mlp_w2, final_norm_w):
    bsz, s_len, d = x.shape
    lc = ctx.shape[1]
    depth = ada_w.shape[0]
    gw = d // 4
    nh = gw // HEAD_DIM
    ng = N_DIR * nh
    kvw = SWA_KV_HEADS * HEAD_DIM
    tm = 2 * LANES
    assert lc % tm == 0 and s_len % tm == 0 and s_len >= 3 * SWA_BLOCK and 2 * kvw == gw
    assert 4 * ng <= GATE_W

    sizes = (gw, gw, gw, gw, ng, ng, gw, kvw, kvw, gw, gw, gw, gw, gw, gw, ng, ng)
    offs = [0]
    for sz in sizes:
        offs.append(offs[-1] + sz)
    seg = lambda i: w_in[:, :, offs[i]:offs[i + 1]]
    gate_cols = jnp.concatenate([seg(4), seg(5), seg(15), seg(16)], axis=-1)
    gate_cols = jnp.pad(gate_cols, ((0, 0), (0, 0), (0, GATE_W - 4 * ng)))
    w_in_p = jnp.concatenate([seg(0), seg(1), seg(2), seg(3), seg(6), seg(7), seg(8), seg(9), seg(10),
                              seg(11), seg(12), seg(13), seg(14), gate_cols], axis=-1).astype(BF16)
    gate_blk = (w_in_p.shape[-1] - GATE_W) // GATE_W
    w_out_b, w1_b, w2_b = w_out.astype(BF16), mlp_w1.astype(BF16), mlp_w2.astype(BF16)

    alog_vec = _lane_vec([(0, gdn_a_log)], depth)
    gate_bias = _lane_vec([(0, gdn_dt_bias), (2 * ng, mlstm_ig_bias), (3 * ng, mlstm_fg_bias)], depth)
    cos_t, sin_t = _rope_tables(s_len, nh)
    b_s_t = jnp.swapaxes(gmlp_b_s, 1, 2)

    rows = -(-(bsz + 1) // SUBLANES) * SUBLANES
    cvec = jnp.concatenate([c, c_ctx[None, :], jnp.zeros((rows - bsz - 1, d), F32)], axis=0)
    mod_all = _adaln(cvec, ada_w, ada_b)
    mod_x = mod_all[:, :bsz].reshape(depth, bsz, 1, N_MOD, d)
    mod_c = jnp.broadcast_to(mod_all[:, bsz].reshape(depth, 1, 1, N_MOD, d), mod_x.shape)
    mod = jnp.concatenate([mod_c, mod_x], axis=2)

    n1 = norm1_w.reshape(depth, 1, d)
    n2 = norm2_w.reshape(depth, 1, d)
    gdn_nw = gdn_norm_w.reshape(depth, 1, HEAD_DIM)
    gmlp_nw = gmlp_norm_w.reshape(depth, 1, gw)
    mlstm_nw = mlstm_norm_w.reshape(depth, 1, gw)

    xs = jnp.concatenate([ctx, x], axis=1)
    for l in range(depth):
        z = _inproj(xs, mod, n1, w_in_p, l, lc, tm)
        mix_a = _gdn(z, alog_vec, gate_bias, gdn_conv_w, gdn_nw, l, lc, gw, gate_blk)
        mix_b = _swa(z, swa_sink, cos_t, sin_t, l, lc, gw, 4, 5)
        mix_c = _gmlp(z, gmlp_w_s, b_s_t, gmlp_nw, l, gw, 6)
        mix_d = _mlstm(z, gate_bias, mlstm_nw, l, lc, gw, 8, gate_blk)
        xs = _outmlp(xs, (mix_a, mix_b, mix_c, mix_d), mod, n2, final_norm_w.reshape(1, d),
                     w_out_b, w1_b, w2_b, l, lc, tm, final=l == depth - 1)
    return xs
```

```python
import functools

import jax
import jax.numpy as jnp
from jax import lax
from jax.experimental import pallas as pl
from jax.experimental.pallas import tpu as pltpu

F32 = jnp.float32
BF16 = jnp.bfloat16

HEAD_DIM = 64
N_DIR = 2
N_MOD = 6
NORM_EPS = 1e-6
SCAN_CHUNK = 64
GDN_CONV = 5
SWA_KV_HEADS = 2
SWA_WINDOW = 128
SWA_BLOCK = 128
GMLP_CHUNK = 128
GRID_W = 64
ROPE_THETA = 10000.0

LANES = 128
SUBLANES = 8
GATE_W = LANES
NEG_BIG = -1e30
VMEM_LIMIT = 56 * 1024 * 1024
SOLVE_PIECES = 3


def _dot(a, b):
    return jnp.dot(a.astype(BF16), b.astype(BF16), preferred_element_type=F32)


def _dot_nt(a, b):
    return lax.dot_general(a.astype(BF16), b.astype(BF16), (((1,), (1,)), ((), ())),
                           preferred_element_type=F32)


def _dot_tn(a, b):
    return lax.dot_general(a.astype(BF16), b.astype(BF16), (((0,), (0,)), ((), ())),
                           preferred_element_type=F32)


def _split_bf16(x, parts):
    out = []
    for _ in range(parts - 1):
        hi = x.astype(BF16)
        out.append(hi)
        x = x - hi.astype(F32)
    out.append(x.astype(BF16))
    return out


def _dot_mask(mask, x):
    acc = None
    for piece in _split_bf16(x, 3):
        term = jnp.dot(mask, piece, preferred_element_type=F32)
        acc = term if acc is None else acc + term
    return acc


def _dot_mask_r(x, mask):
    acc = None
    for piece in _split_bf16(x, 3):
        term = jnp.dot(piece, mask, preferred_element_type=F32)
        acc = term if acc is None else acc + term
    return acc


_NN = (((1,), (0,)), ((), ()))
_NT = (((1,), (1,)), ((), ()))
_TN = (((0,), (0,)), ((), ()))


def _dot_pieces(a, b, dims=_NN):
    dg = lambda x, y: lax.dot_general(x, y, dims, preferred_element_type=F32)
    out = dg(a[0], b[0]) + (dg(a[0], b[1]) + dg(a[1], b[0]))
    if len(a) == 3:
        out = out + ((dg(a[0], b[2]) + dg(a[2], b[0])) + dg(a[1], b[1]))
    return out


def _dot_solve(a, b, dims=_NN):
    return _dot_pieces(_split_bf16(a, 2), _split_bf16(b, 2), dims)


def _params(*sem):
    return pltpu.CompilerParams(dimension_semantics=sem, vmem_limit_bytes=VMEM_LIMIT)


def _rms(x, n):
    return x * lax.rsqrt(jnp.sum(x * x, axis=-1, keepdims=True) * (1.0 / n) + NORM_EPS)


def _tri_masks(c):
    row = lax.broadcasted_iota(jnp.int32, (c, c), 0)
    col = lax.broadcasted_iota(jnp.int32, (c, c), 1)
    incl = (row >= col, row <= col)
    strict = (row > col, row < col)
    return incl, strict


def _scan_batch(b):
    return 2 if b % 2 == 0 else 1


def _scan_chunk_index(i, d, ncc, nc):
    if d == 0:
        return i
    return jnp.where(i < ncc, ncc - 1 - i, ncc + nc - 1 - i)


def _adaln_kernel(c_ref, w_ref, b_ref, o_ref):
    o_ref[...] = _dot(jax.nn.silu(c_ref[...]), w_ref[...]) + b_ref[...]


def _adaln(cvec, ada_w, ada_b):
    depth, d, n = ada_w.shape
    rows = cvec.shape[0]
    bn = n // N_MOD
    return pl.pallas_call(
        _adaln_kernel,
        grid=(depth, n // bn),
        in_specs=[pl.BlockSpec((rows, d), lambda l, j: (0, 0)),
                  pl.BlockSpec((None, d, bn), lambda l, j: (l, 0, j)),
                  pl.BlockSpec((None, 1, bn), lambda l, j: (l, 0, j))],
        out_specs=pl.BlockSpec((None, rows, bn), lambda l, j: (l, 0, j)),
        out_shape=jax.ShapeDtypeStruct((depth, rows, n), F32),
        compiler_params=_params("arbitrary", "arbitrary"),
        name="adaln",
    )(cvec, ada_w, ada_b.reshape(depth, 1, n))


def _inproj_kernel(x_ref, mod_ref, nw_ref, w_ref, z_ref):
    x = x_ref[...]
    h = _rms(x, x.shape[-1]) * nw_ref[...]
    h = h * (1.0 + mod_ref[1:2, :]) + mod_ref[0:1, :]
    z_ref[...] = _dot(h, w_ref[...])


def _inproj(xs, mod, norm_w, w_in, layer, lc, tm):
    b, t, d = xs.shape
    n = w_in.shape[-1]
    nct = lc // tm
    return pl.pallas_call(
        _inproj_kernel,
        grid=(b, t // tm),
        in_specs=[pl.BlockSpec((None, tm, d), lambda bi, i: (bi, i, 0)),
                  pl.BlockSpec((None, None, None, N_MOD, d),
                               lambda bi, i: (layer, bi, jnp.where(i >= nct, 1, 0), 0, 0)),
                  pl.BlockSpec((None, 1, d), lambda bi, i: (layer, 0, 0)),
                  pl.BlockSpec((None, d, n), lambda bi, i: (layer, 0, 0))],
        out_specs=pl.BlockSpec((None, tm, n), lambda bi, i: (bi, i, 0)),
        out_shape=jax.ShapeDtypeStruct((b, t, n), F32),
        compiler_params=_params("arbitrary", "arbitrary"),
        name="inproj",
    )(xs, mod, norm_w, w_in)


def _gdn_kernel(alog_ref, gb_ref, q_ref, k_ref, v_ref, z_ref, g_ref, cw_ref, nw_ref, o_ref,
                of_ref, ob_ref, s_ref, *, lc, t):
    c = SCAN_CHUNK
    nb = q_ref.shape[0]
    gw = q_ref.shape[-1]
    nh = gw // HEAD_DIM
    nc, ncc = t // c, lc // c
    assert c >= 4 and c & (c - 1) == 0
    incl, strict = _tri_masks(c)
    inclb = [jnp.where(m, 1.0, 0.0).astype(BF16) for m in incl]
    eye = jnp.where(incl[0] & incl[1], 1.0, 0.0)
    diag_blk = (lax.broadcasted_iota(jnp.int32, (c, c), 0) // 16
                == lax.broadcasted_iota(jnp.int32, (c, c), 1) // 16)
    neg_a = -jnp.exp(alog_ref[...])
    head_ones = jnp.where(lax.broadcasted_iota(jnp.int32, (gw, gw), 0) // HEAD_DIM
                          == lax.broadcasted_iota(jnp.int32, (gw, gw), 1) // HEAD_DIM,
                          1.0, 0.0).astype(BF16)
    s_ref[...] = jnp.zeros_like(s_ref)

    def prep(bb, ci):
        r0 = pl.multiple_of(ci * c, c)
        lo = pl.multiple_of(jnp.maximum(r0 - SUBLANES, 0), SUBLANES)
        hi = pl.multiple_of(jnp.minimum(r0 + c, t - SUBLANES), SUBLANES)
        tpos = r0 + lax.broadcasted_iota(jnp.int32, (c, 1), 0)
        seg_lo = jnp.where(tpos < lc, 0, lc)
        seg_hi = jnp.where(tpos < lc, lc, t)
        half = GDN_CONV // 2
        outs = []
        for idx, ref in enumerate((q_ref, k_ref, v_ref)):
            x = jnp.concatenate([ref[bb, pl.ds(lo, SUBLANES), :], ref[bb, pl.ds(r0, c), :],
                                 ref[bb, pl.ds(hi, SUBLANES), :]], axis=0)
            acc = None
            for tap in range(GDN_CONV):
                dlt = tap - half
                xs = x[SUBLANES + dlt:SUBLANES + dlt + c]
                if dlt != 0:
                    ok = jnp.where((tpos + dlt >= seg_lo) & (tpos + dlt < seg_hi), 1.0, 0.0)
                    xs = jnp.where(ok > 0.5, xs, 0.0)
                term = xs * cw_ref[tap:tap + 1, idx * gw:(idx + 1) * gw]
                acc = term if acc is None else acc + term
            outs.append(jax.nn.silu(acc))
        qn = outs[0] * (lax.rsqrt(_dot_mask_r(outs[0] * outs[0], head_ones) + NORM_EPS)
                        * HEAD_DIM ** -0.5)
        kn = outs[1] * lax.rsqrt(_dot_mask_r(outs[1] * outs[1], head_ones) + NORM_EPS)
        heads = [slice(h * HEAD_DIM, (h + 1) * HEAD_DIM) for h in range(nh)]
        return (r0, [qn[:, sl] for sl in heads], [kn[:, sl] for sl in heads],
                [outs[2][:, sl] for sl in heads], g_ref[bb, pl.ds(r0, c), :])

    def step(i, carry):
        chains = []
        rows = []
        for bb, d in [(bb, d) for bb in range(nb) for d in range(N_DIR)]:
            r0, qs, ks, vs, gt = prep(bb, _scan_chunk_index(i, d, ncc, nc))
            rows.append(r0)
            gval = neg_a * jax.nn.softplus(gt + gb_ref[...])
            beta = jax.nn.sigmoid(gt)
            gcum = _dot_mask(inclb[d], gval)
            gtot = gcum[c - 1:c, :] if d == 0 else gcum[0:1, :]
            gcum_t = gcum.T
            egc = jnp.exp(gcum)
            ekd = jnp.exp(gtot - gcum)
            egl = jnp.exp(gtot)
            for h in range(nh):
                col = d * nh + h
                bcol = N_DIR * nh + col
                kb = ks[h] * beta[:, bcol:bcol + 1]
                chains.append(dict(
                    bb=bb, d=d, slot=bb * N_DIR * nh + col, q=qs[h], k=ks[h], kb=kb,
                    decay=jnp.exp(jnp.where(incl[d], gcum[:, col:col + 1] - gcum_t[col:col + 1, :],
                                            NEG_BIG)),
                    rhs=jnp.concatenate([vs[h] * beta[:, bcol:bcol + 1], kb * egc[:, col:col + 1]],
                                        axis=1),
                    q_dec=qs[h] * egc[:, col:col + 1], k_dec=ks[h] * ekd[:, col:col + 1],
                    g_last=egl[:, col:col + 1]))
        for ch in chains:
            ch["both"] = _dot_nt(jnp.concatenate([ch["kb"], ch["q"]], axis=0), ch["k"])
        for ch in chains:
            ch["attn"] = ch["both"][c:] * ch["decay"]
            ch["n"] = jnp.where(strict[ch["d"]], -(ch["both"][:c] * ch["decay"]), 0.0)
            ch["p"] = eye + ch["n"]
        pcs = 2
        mm = lambda x, y: _dot_pieces(_split_bf16(x, pcs), _split_bf16(y, pcs))
        for ch in chains:
            ch["nl"] = jnp.where(diag_blk, 0.0, ch["n"])
            ch["n"] = jnp.where(diag_blk, ch["n"], 0.0)
            ch["p"] = eye + ch["n"]
        for ch in chains:
            ch["n"] = mm(ch["n"], ch["n"])
        for _ in range(2):
            for ch in chains:
                nxt = mm(jnp.concatenate([ch["n"], ch["p"]], axis=0), ch["n"])
                ch["p"] = ch["p"] + nxt[c:]
                ch["n"] = nxt[:c]
        for ch in chains:
            ch["td"] = ch["p"] + mm(ch["p"], ch["n"])
        for ch in chains:
            ch["mneg"] = mm(ch["td"], ch["nl"])
            ch["y"] = mm(ch["td"], ch["rhs"])
        for ch in chains:
            ch["m2"] = mm(ch["mneg"], ch["mneg"])
        for ch in chains:
            ch["r"] = (eye + ch["mneg"]) + mm(eye + ch["mneg"], ch["m2"])
        for ch in chains:
            ch["sol"] = mm(ch["r"], ch["y"])
        for ch in chains:
            ch["state"] = s_ref[ch["slot"]]
            ch["ws"] = _dot(jnp.concatenate([ch["sol"][:, HEAD_DIM:], ch["q_dec"]], axis=0), ch["state"])
        for ch in chains:
            ch["v_new"] = ch["sol"][:, :HEAD_DIM] - ch["ws"][:c]
            ch["o"] = ch["ws"][c:] + _dot(ch["attn"], ch["v_new"])
        for ch in chains:
            s_ref[ch["slot"]] = ch["state"] * ch["g_last"] + _dot_tn(ch["k_dec"], ch["v_new"])
        for bb in range(nb):
            for d, ref in enumerate((of_ref, ob_ref)):
                ref[bb, pl.ds(rows[bb * N_DIR + d], c), :] = jnp.concatenate(
                    [ch["o"] for ch in chains if ch["d"] == d and ch["bb"] == bb], axis=1)
        return carry

    lax.fori_loop(0, nc, step, 0)

    rb = 2 * LANES

    def finish(r, carry):
        r0 = pl.multiple_of(r * rb, rb)
        for bb in range(nb):
            o = of_ref[bb, pl.ds(r0, rb), :] + ob_ref[bb, pl.ds(r0, rb), :]
            parts = []
            for h in range(nh):
                parts.append(_rms(o[:, h * HEAD_DIM:(h + 1) * HEAD_DIM], HEAD_DIM) * nw_ref[...])
            o_ref[bb, pl.ds(r0, rb), :] = (jnp.concatenate(parts, axis=1)
                                           * jax.nn.silu(z_ref[bb, pl.ds(r0, rb), :]))
        return carry

    lax.fori_loop(0, t // rb, finish, 0)


def _gdn(z, alog_vec, gate_bias, conv_w, norm_w, layer, lc, gw, gate_blk):
    b, t, _ = z.shape
    nh = gw // HEAD_DIM
    nb = _scan_batch(b)
    col = lambda j: pl.BlockSpec((nb, t, gw), lambda bi: (bi, 0, j), pipeline_mode=pl.Buffered(1))
    return pl.pallas_call(
        functools.partial(_gdn_kernel, lc=lc, t=t),
        grid=(b // nb,),
        in_specs=[pl.BlockSpec((None, 1, GATE_W), lambda bi: (layer, 0, 0)),
                  pl.BlockSpec((None, 1, GATE_W), lambda bi: (layer, 0, 0)),
                  col(0), col(1), col(2), col(3),
                  pl.BlockSpec((nb, t, GATE_W), lambda bi: (bi, 0, gate_blk),
                               pipeline_mode=pl.Buffered(1)),
                  pl.BlockSpec((None, GDN_CONV, 3 * gw), lambda bi: (layer, 0, 0)),
                  pl.BlockSpec((None, 1, HEAD_DIM), lambda bi: (layer, 0, 0))],
        out_specs=pl.BlockSpec((nb, t, gw), lambda bi: (bi, 0, 0)),
        out_shape=jax.ShapeDtypeStruct((b, t, gw), F32),
        scratch_shapes=[pltpu.VMEM((nb, t, gw), F32), pltpu.VMEM((nb, t, gw), F32),
                        pltpu.VMEM((nb * N_DIR * nh, HEAD_DIM, HEAD_DIM), F32)],
        compiler_params=_params("arbitrary"),
        name="gdn",
    )(alog_vec, gate_bias, z, z, z, z, z, conv_w, norm_w)


def _mlstm_kernel(gb_ref, q_ref, k_ref, v_ref, og_ref, g_ref, nw_ref, o_ref,
                  hf_ref, hb_ref, cn_ref, m_ref, *, lc, t):
    c = SCAN_CHUNK
    nb = q_ref.shape[0]
    gw = q_ref.shape[-1]
    nh = gw // HEAD_DIM
    nc, ncc = t // c, lc // c
    assert c == HEAD_DIM
    icol = 2 * N_DIR * nh
    fcol = 3 * N_DIR * nh

    ii = lax.broadcasted_iota(jnp.int32, (c, gw), 0)
    jj = lax.broadcasted_iota(jnp.int32, (c, gw), 1)
    tri = (ii >= jj % HEAD_DIM, ii <= jj % HEAD_DIM)
    eye4 = jnp.where(ii == jj % HEAD_DIM, 1.0, 0.0)
    ri = lax.broadcasted_iota(jnp.int32, (c, c), 0)
    ci = lax.broadcasted_iota(jnp.int32, (c, c), 1)
    inclb = [jnp.where(ri >= ci, 1.0, 0.0).astype(BF16), jnp.where(ri <= ci, 1.0, 0.0).astype(BF16)]
    ones_cc = jnp.ones((c, c), BF16)
    bi_ = lax.broadcasted_iota(jnp.int32, (gw, gw), 0) // HEAD_DIM
    bj_ = lax.broadcasted_iota(jnp.int32, (gw, gw), 1) // HEAD_DIM
    block = bi_ == bj_
    block_ones = jnp.where(block, 1.0, 0.0).astype(BF16)
    block2 = jnp.concatenate([block, block], axis=1)
    gl = lax.broadcasted_iota(jnp.int32, (GATE_W, gw), 0)
    gh = lax.broadcasted_iota(jnp.int32, (GATE_W, gw), 1) // HEAD_DIM
    sel_f = [jnp.where(gl == fcol + d * nh + gh, 1.0, 0.0).astype(BF16) for d in range(N_DIR)]
    sel_i = [jnp.where(gl == icol + d * nh + gh, 1.0, 0.0).astype(BF16) for d in range(N_DIR)]
    ones_v = jnp.ones((c, gw), F32)

    cn_ref[...] = jnp.zeros_like(cn_ref)
    m_ref[...] = jnp.zeros_like(m_ref)

    def step(i, carry):
        units = [(bb, d) for bb in range(nb) for d in range(N_DIR)]
        es = range(len(units))
        r0 = [pl.multiple_of(_scan_chunk_index(i, d, ncc, nc) * c, c) for _, d in units]
        gt = [g_ref[bb, pl.ds(r0[e], c), :] + gb_ref[...] for e, (bb, _) in enumerate(units)]
        q_all = [q_ref[bb, pl.ds(r0[e], c), :] * HEAD_DIM ** -0.5 for e, (bb, _) in enumerate(units)]
        k_all = [k_ref[bb, pl.ds(r0[e], c), :] for e, (bb, _) in enumerate(units)]
        v_all = [v_ref[bb, pl.ds(r0[e], c), :] for e, (bb, _) in enumerate(units)]
        k_bd = [jnp.where(block, jnp.concatenate([k_all[e]] * nh, axis=0), 0.0) for e in es]
        qk = [_dot_nt(q_all[e], k_bd[e]) for e in es]
        cn = [cn_ref[e] for e in es]
        qcn = [_dot(q_all[e], cn[e]) for e in es]
        bcum = [_dot_mask(inclb[d], jax.nn.log_sigmoid(gt[e])) for e, (_, d) in enumerate(units)]
        b_col = [_dot_mask_r(bcum[e], sel_f[d]) for e, (_, d) in enumerate(units)]
        ig_col = [_dot_mask_r(gt[e], sel_i[d]) for e, (_, d) in enumerate(units)]
        x_col = [ig_col[e] - b_col[e] for e in es]
        x_row = [_dot_mask(ones_cc, x_col[e] * eye4) for e in es]
        b_row = [_dot_mask(ones_cc, b_col[e] * eye4) for e in es]
        m_old = [m_ref[e] for e in es]
        b_last = [b_col[e][c - 1:c, :] if d == 0 else b_col[e][0:1, :] for e, (_, d) in enumerate(units)]
        wst = [b_last[e] - b_col[e] + ig_col[e] for e in es]
        m_new = [jnp.maximum(b_last[e] + m_old[e], jnp.max(wst[e], axis=0, keepdims=True)) for e in es]
        k_w = [k_all[e] * jnp.exp(wst[e] - m_new[e]) for e in es]
        upd = [_dot_tn(k_w[e], jnp.concatenate([v_all[e], ones_v], axis=1)) for e in es]
        for e in es:
            c_decay = jnp.exp(b_last[e] + m_old[e] - m_new[e])
            cn_ref[e] = (jnp.concatenate([c_decay, c_decay], axis=1) * cn[e]
                         + jnp.where(block2, upd[e], 0.0))
            m_ref[e] = m_new[e]
        dmat_t = [jnp.where(tri[1 - d], b_row[e] + x_col[e], NEG_BIG) for e, (_, d) in enumerate(units)]
        mt_row = [jnp.maximum(b_row[e][0:1, :] + m_old[e], jnp.max(dmat_t[e], axis=0, keepdims=True))
                  for e in es]
        m_t = [_dot_mask_r(mt_row[e] * eye4, block_ones) for e in es]
        s = [qk[e] * jnp.exp(jnp.where(tri[d], b_col[e] + x_row[e], NEG_BIG) - m_t[e])
             for e, (_, d) in enumerate(units)]
        v_bd = [jnp.where(block, jnp.concatenate([v_all[e]] * nh, axis=0), 0.0) for e in es]
        sv = [_dot(s[e], jnp.concatenate([v_bd[e], block_ones.astype(F32)], axis=1)) for e in es]
        for e, (bb, d) in enumerate(units):
            ref = hf_ref if d == 0 else hb_ref
            w_inter = jnp.exp(b_col[e] + m_old[e] - m_t[e])
            tot = jnp.concatenate([w_inter, w_inter], axis=1) * qcn[e] + sv[e]
            ref[bb, pl.ds(r0[e], c), :] = (tot[:, :gw]
                                           / jnp.maximum(jnp.abs(tot[:, gw:]), jnp.exp(-m_t[e])))
        return carry

    lax.fori_loop(0, nc, step, 0)

    rb = 2 * LANES

    def finish(r, carry):
        r0 = pl.multiple_of(r * rb, rb)
        for bb in range(nb):
            hsum = hf_ref[bb, pl.ds(r0, rb), :] + hb_ref[bb, pl.ds(r0, rb), :]
            parts = [_rms(hsum[:, h * HEAD_DIM:(h + 1) * HEAD_DIM], HEAD_DIM) for h in range(nh)]
            o_ref[bb, pl.ds(r0, rb), :] = (jax.nn.sigmoid(og_ref[bb, pl.ds(r0, rb), :])
                                           * (jnp.concatenate(parts, axis=1) * nw_ref[...]))
        return carry

    lax.fori_loop(0, t // rb, finish, 0)


def _mlstm(z, gate_bias, norm_w, layer, lc, gw, first_blk, gate_blk):
    b, t, _ = z.shape
    nb = _scan_batch(b)
    col = lambda j: pl.BlockSpec((nb, t, gw), lambda bi: (bi, 0, first_blk + j),
                                 pipeline_mode=pl.Buffered(1))
    return pl.pallas_call(
        functools.partial(_mlstm_kernel, lc=lc, t=t),
        grid=(b // nb,),
        in_specs=[pl.BlockSpec((None, 1, GATE_W), lambda bi: (layer, 0, 0)),
                  col(0), col(1), col(2), col(3),
                  pl.BlockSpec((nb, t, GATE_W), lambda bi: (bi, 0, gate_blk),
                               pipeline_mode=pl.Buffered(1)),
                  pl.BlockSpec((None, 1, gw), lambda bi: (layer, 0, 0))],
        out_specs=pl.BlockSpec((nb, t, gw), lambda bi: (bi, 0, 0)),
        out_shape=jax.ShapeDtypeStruct((b, t, gw), F32),
        scratch_shapes=[pltpu.VMEM((nb, t, gw), F32), pltpu.VMEM((nb, t, gw), F32),
                        pltpu.VMEM((nb * N_DIR, gw, 2 * gw), F32),
                        pltpu.VMEM((nb * N_DIR, 1, gw), F32)],
        compiler_params=_params("arbitrary"),
        name="mlstm",
    )(gate_bias, z, z, z, z, z, norm_w)


def _swa_kernel(sink_ref, q_ref, kv_ref, cos_ref, sin_ref, o_ref, qr_ref, kr_ref, *, lc, t, layer):
    gw = q_ref.shape[-1]
    kvw = SWA_KV_HEADS * HEAD_DIM
    grp = gw // kvw
    s_len = t - lc
    blk = SWA_BLOCK
    band = 3 * blk
    scale = HEAD_DIM ** -0.5
    half = HEAD_DIM // 2

    qr_ref[0:lc, :] = q_ref[0:lc, :] * scale
    kr_ref[0:lc, :] = kv_ref[0:lc, 0:kvw]

    lane = lax.broadcasted_iota(jnp.int32, (1, gw), 1)
    first_half = (lane % HEAD_DIM) < half

    def rope(x, cs, sn):
        w = x.shape[-1]
        rot = jnp.where(first_half[:, :w], pltpu.roll(x, w - half, 1), pltpu.roll(x, half, 1))
        return x * cs + rot * sn

    rb = 2 * LANES

    def rope_rows(r, carry):
        p0 = pl.multiple_of(r * rb, rb)
        r0 = pl.multiple_of(lc + r * rb, rb)
        cs, sn = cos_ref[pl.ds(p0, rb), :], sin_ref[pl.ds(p0, rb), :]
        qr_ref[pl.ds(r0, rb), :] = rope(q_ref[pl.ds(r0, rb), :], cs, sn) * scale
        kr_ref[pl.ds(r0, rb), :] = rope(kv_ref[pl.ds(r0, rb), 0:kvw], cs[:, :kvw], sn[:, :kvw])
        return carry

    lax.fori_loop(0, s_len // rb, rope_rows, 0)

    def sink_col(kvh, rows):
        ridx = lax.broadcasted_iota(jnp.int32, (grp * rows, 1), 0)
        out = jnp.full((grp * rows, 1), sink_ref[layer, kvh * grp], F32)
        for g in range(1, grp):
            out = jnp.where(ridx >= g * rows, sink_ref[layer, kvh * grp + g], out)
        return out

    def stack_heads(qb, kvh):
        return jnp.concatenate([qb[:, (kvh * grp + g) * HEAD_DIM:(kvh * grp + g + 1) * HEAD_DIM]
                                for g in range(grp)], axis=0)

    def unstack_heads(o_list, rows):
        return jnp.concatenate([o[g * rows:(g + 1) * rows] for o in o_list for g in range(grp)], axis=1)

    outs = []
    for kvh in range(SWA_KV_HEADS):
        hs = slice(kvh * HEAD_DIM, (kvh + 1) * HEAD_DIM)
        q2 = stack_heads(qr_ref[0:lc, :], kvh)
        s = _dot_nt(q2, kr_ref[0:lc, hs])
        sk = sink_col(kvh, lc)
        m = jnp.maximum(jnp.max(s, axis=-1, keepdims=True), sk)
        e = jnp.exp(s - m)
        den = jnp.exp(sk - m) + jnp.sum(e, axis=-1, keepdims=True)
        outs.append(_dot(e, kv_ref[0:lc, kvw + kvh * HEAD_DIM:kvw + (kvh + 1) * HEAD_DIM]) / den)
    o_ref[0:lc, :] = unstack_heads(outs, lc)

    def block(n, carry):
        qrow = pl.multiple_of(lc + n * blk, blk)
        boff = pl.multiple_of(jnp.clip((n - 1) * blk, 0, s_len - band), blk)
        start = pl.multiple_of(lc + boff, blk)
        qb = qr_ref[pl.ds(qrow, blk), :]
        kb = kr_ref[pl.ds(start, band), :]
        vb = kv_ref[pl.ds(start, band), kvw:2 * kvw]
        qpos = n * blk + lax.broadcasted_iota(jnp.int32, (grp * blk, band), 0) % blk
        kpos = boff + lax.broadcasted_iota(jnp.int32, (grp * blk, band), 1)
        mask = jnp.abs(qpos - kpos) <= SWA_WINDOW
        kvs = range(SWA_KV_HEADS)
        hs = [slice(kvh * HEAD_DIM, (kvh + 1) * HEAD_DIM) for kvh in kvs]
        q2 = [stack_heads(qb, kvh) for kvh in kvs]
        sb = [jnp.where(mask, _dot_nt(q2[kvh], kb[:, hs[kvh]]), NEG_BIG) for kvh in kvs]
        sx = [_dot_nt(q2[kvh], kr_ref[0:lc, hs[kvh]]) for kvh in kvs]
        sk = [sink_col(kvh, blk) for kvh in kvs]
        m = [jnp.maximum(jnp.maximum(jnp.max(sb[kvh], axis=-1, keepdims=True),
                                     jnp.max(sx[kvh], axis=-1, keepdims=True)), sk[kvh]) for kvh in kvs]
        eb = [jnp.exp(sb[kvh] - m[kvh]) for kvh in kvs]
        ex = [jnp.exp(sx[kvh] - m[kvh]) for kvh in kvs]
        den = [(jnp.exp(sk[kvh] - m[kvh]) + jnp.sum(eb[kvh], axis=-1, keepdims=True)
                + jnp.sum(ex[kvh], axis=-1, keepdims=True)) for kvh in kvs]
        pv = [(_dot(eb[kvh], vb[:, hs[kvh]])
               + _dot(ex[kvh], kv_ref[0:lc, kvw + kvh * HEAD_DIM:kvw + (kvh + 1) * HEAD_DIM]))
              for kvh in kvs]
        o_ref[pl.ds(qrow, blk), :] = unstack_heads([pv[kvh] / den[kvh] for kvh in kvs], blk)
        return carry

    lax.fori_loop(0, s_len // blk, block, 0)


def _swa(z, sink, cos_t, sin_t, layer, lc, gw, q_blk, kv_blk):
    b, t, _ = z.shape
    s_len = t - lc
    kvw = SWA_KV_HEADS * HEAD_DIM
    return pl.pallas_call(
        functools.partial(_swa_kernel, lc=lc, t=t, layer=layer),
        grid=(b,),
        in_specs=[pl.BlockSpec(memory_space=pltpu.SMEM),
                  pl.BlockSpec((None, t, gw), lambda bi: (bi, 0, q_blk)),
                  pl.BlockSpec((None, t, 2 * kvw), lambda bi: (bi, 0, kv_blk)),
                  pl.BlockSpec((s_len, gw), lambda bi: (0, 0)),
                  pl.BlockSpec((s_len, gw), lambda bi: (0, 0))],
        out_specs=pl.BlockSpec((None, t, gw), lambda bi: (bi, 0, 0)),
        out_shape=jax.ShapeDtypeStruct((b, t, gw), F32),
        scratch_shapes=[pltpu.VMEM((t, gw), F32), pltpu.VMEM((t, kvw), F32)],
        compiler_params=_params("arbitrary"),
        name="swa",
    )(sink, z, z, cos_t, sin_t)


def _gmlp_kernel(u_ref, v_ref, ws_ref, bs_ref, nw_ref, o_ref, *, t):
    gw = u_ref.shape[-1]
    ng = ws_ref.shape[0]
    cw = gw // ng
    ck = GMLP_CHUNK

    def chunk(ci, carry):
        r0 = pl.multiple_of(ci * ck, ck)
        u = jax.nn.gelu(u_ref[pl.ds(r0, ck), :])
        v = _rms(jax.nn.gelu(v_ref[pl.ds(r0, ck), :]), gw) * nw_ref[...]
        parts = [_dot(ws_ref[g], v[:, g * cw:(g + 1) * cw]) + bs_ref[:, g:g + 1] for g in range(ng)]
        o_ref[pl.ds(r0, ck), :] = u * jnp.concatenate(parts, axis=1)
        return carry

    lax.fori_loop(0, t // ck, chunk, 0)


def _gmlp(z, w_s, b_s_t, norm_w, layer, gw, u_blk):
    b, t, _ = z.shape
    ng = w_s.shape[1]
    return pl.pallas_call(
        functools.partial(_gmlp_kernel, t=t),
        grid=(b,),
        in_specs=[pl.BlockSpec((None, t, gw), lambda bi: (bi, 0, u_blk)),
                  pl.BlockSpec((None, t, gw), lambda bi: (bi, 0, u_blk + 1)),
                  pl.BlockSpec((None, ng, GMLP_CHUNK, GMLP_CHUNK), lambda bi: (layer, 0, 0, 0)),
                  pl.BlockSpec((None, GMLP_CHUNK, ng), lambda bi: (layer, 0, 0)),
                  pl.BlockSpec((None, 1, gw), lambda bi: (layer, 0, 0))],
        out_specs=pl.BlockSpec((None, t, gw), lambda bi: (bi, 0, 0)),
        out_shape=jax.ShapeDtypeStruct((b, t, gw), F32),
        compiler_params=_params("arbitrary"),
        name="gmlp",
    )(z, z, w_s, b_s_t, norm_w)


def _outmlp_kernel(x_ref, a_ref, b_ref, c_ref, d_ref, mod_ref, nw_ref, fw_ref, wo_ref, w1_ref, w2_ref,
                   o_ref, *, ff_blk, final):
    gw = a_ref.shape[-1]
    acc = None
    for g, ref in enumerate((a_ref, b_ref, c_ref, d_ref)):
        term = _dot(ref[...], wo_ref[g * gw:(g + 1) * gw, :])
        acc = term if acc is None else acc + term
    x1 = x_ref[...] + mod_ref[2:3, :] * acc
    h = _rms(x1, x1.shape[-1]) * nw_ref[...]
    h = (h * (1.0 + mod_ref[4:5, :]) + mod_ref[3:4, :]).astype(BF16)
    y = None
    for j in range(w1_ref.shape[-1] // ff_blk):
        hid = jnp.square(jnp.maximum(_dot(h, w1_ref[:, j * ff_blk:(j + 1) * ff_blk]), 0.0))
        term = _dot(hid, w2_ref[j * ff_blk:(j + 1) * ff_blk, :])
        y = term if y is None else y + term
    x2 = x1 + mod_ref[5:6, :] * y
    o_ref[...] = _rms(x2, x2.shape[-1]) * fw_ref[...] if final else x2


def _outmlp(xs, mixes, mod, norm_w, final_w, w_out, w1, w2, layer, lc, tm, final):
    b, t, d = xs.shape
    gw = mixes[0].shape[-1]
    dff = w1.shape[-1]
    nct = lc // tm
    off = nct if final else 0
    rows = t - off * tm
    tile = lambda w: pl.BlockSpec((None, tm, w), lambda bi, i: (bi, i + off, 0))
    const = lambda r, c: pl.BlockSpec((None, r, c), lambda bi, i: (layer, 0, 0),
                                      pipeline_mode=pl.Buffered(1))
    return pl.pallas_call(
        functools.partial(_outmlp_kernel, ff_blk=min(dff, 1024), final=final),
        grid=(b, rows // tm),
        in_specs=[tile(d), tile(gw), tile(gw), tile(gw), tile(gw),
                  pl.BlockSpec((None, None, None, N_MOD, d),
                               lambda bi, i: (layer, bi, jnp.where(i + off >= nct, 1, 0), 0, 0)),
                  pl.BlockSpec((None, 1, d), lambda bi, i: (layer, 0, 0)),
                  pl.BlockSpec((1, d), lambda bi, i: (0, 0)),
                  const(d, d), const(d, dff), const(dff, d)],
        out_specs=pl.BlockSpec((None, tm, d), lambda bi, i: (bi, i, 0)),
        out_shape=jax.ShapeDtypeStruct((b, rows, d), F32),
        compiler_params=_params("arbitrary", "arbitrary"),
        name="outmlp",
    )(xs, *mixes, mod, norm_w, final_w, w_out, w1, w2)


def _rope_tables(s_len, n_heads):
    rows = s_len // GRID_W
    row = jnp.repeat(jnp.arange(rows), GRID_W).astype(F32)
    col = (jnp.arange(rows * GRID_W) % GRID_W).astype(F32)
    n_freq = HEAD_DIM // 4
    inv = jnp.power(ROPE_THETA, -jnp.arange(n_freq, dtype=F32) / n_freq)
    ang = jnp.concatenate([row[:, None] * inv, col[:, None] * inv], axis=-1)
    cos, sin = jnp.cos(ang), jnp.sin(ang)
    return (jnp.tile(jnp.concatenate([cos, cos], axis=-1), (1, n_heads)),
            jnp.tile(jnp.concatenate([-sin, sin], axis=-1), (1, n_heads)))


def _lane_vec(parts, depth):
    out = jnp.zeros((depth, 1, GATE_W), F32)
    for off, val in parts:
        val = val.reshape(depth, 1, -1).astype(F32)
        out = lax.dynamic_update_slice(out, val, (0, 0, off))
    return out


def kernel(x, c, ctx, c_ctx, ada_w, ada_b, norm1_w, norm2_w, w_in, w_out, gdn_conv_w, gdn_a_log,
           gdn_dt_bias, gdn_norm_w, swa_sink, gmlp_w_s, gmlp_b_s, gmlp_norm_w, mlstm_ig_bias,
           mlstm_fg_bias, mlstm_norm_w, mlp_w1, mlp_w2, final_norm_w):
    bsz, s_len, d = x.shape
    lc = ctx.shape[1]
    depth = ada_w.shape[0]
    gw = d // 4
    nh = gw // HEAD_DIM
    ng = N_DIR * nh
    kvw = SWA_KV_HEADS * HEAD_DIM
    tm = 2 * LANES
    assert lc % tm == 0 and s_len % tm == 0 and s_len >= 3 * SWA_BLOCK and 2 * kvw == gw
    assert 4 * ng <= GATE_W

    sizes = (gw, gw, gw, gw, ng, ng, gw, kvw, kvw, gw, gw, gw, gw, gw, gw, ng, ng)
    offs = [0]
    for sz in sizes:
        offs.append(offs[-1] + sz)
    seg = lambda i: w_in[:, :, offs[i]:offs[i + 1]]
    gate_cols = jnp.concatenate([seg(4), seg(5), seg(15), seg(16)], axis=-1)
    gate_cols = jnp.pad(gate_cols, ((0, 0), (0, 0), (0, GATE_W - 4 * ng)))
    w_in_p = jnp.concatenate([seg(0), seg(1), seg(2), seg(3), seg(6), seg(7), seg(8), seg(9), seg(10),
                              seg(11), seg(12), seg(13), seg(14), gate_cols], axis=-1).astype(BF16)
    gate_blk = (w_in_p.shape[-1] - GATE_W) // GATE_W
    w_out_b, w1_b, w2_b = w_out.astype(BF16), mlp_w1.astype(BF16), mlp_w2.astype(BF16)

    alog_vec = _lane_vec([(0, gdn_a_log)], depth)
    gate_bias = _lane_vec([(0, gdn_dt_bias), (2 * ng, mlstm_ig_bias), (3 * ng, mlstm_fg_bias)], depth)
    cos_t, sin_t = _rope_tables(s_len, nh)
    b_s_t = jnp.swapaxes(gmlp_b_s, 1, 2)

    rows = -(-(bsz + 1) // SUBLANES) * SUBLANES
    cvec = jnp.concatenate([c, c_ctx[None, :], jnp.zeros((rows - bsz - 1, d), F32)], axis=0)
    mod_all = _adaln(cvec, ada_w, ada_b)
    mod_x = mod_all[:, :bsz].reshape(depth, bsz, 1, N_MOD, d)
    mod_c = jnp.broadcast_to(mod_all[:, bsz].reshape(depth, 1, 1, N_MOD, d), mod_x.shape)
    mod = jnp.concatenate([mod_c, mod_x], axis=2)

    n1 = norm1_w.reshape(depth, 1, d)
    n2 = norm2_w.reshape(depth, 1, d)
    gdn_nw = gdn_norm_w.reshape(depth, 1, HEAD_DIM)
    gmlp_nw = gmlp_norm_w.reshape(depth, 1, gw)
    mlstm_nw = mlstm_norm_w.reshape(depth, 1, gw)

    xs = jnp.concatenate([ctx, x], axis=1)
    for l in range(depth):
        z = _inproj(xs, mod, n1, w_in_p, l, lc, tm)
        mix_a = _gdn(z, alog_vec, gate_bias, gdn_conv_w, gdn_nw, l, lc, gw, gate_blk)
        mix_b = _swa(z, swa_sink, cos_t, sin_t, l, lc, gw, 4, 5)
        mix_c = _gmlp(z, gmlp_w_s, b_s_t, gmlp_nw, l, gw, 6)
        mix_d = _mlstm(z, gate_bias, mlstm_nw, l, lc, gw, 8, gate_blk)
        xs = _outmlp(xs, (mix_a, mix_b, mix_c, mix_d), mod, n2, final_norm_w.reshape(1, d),
                     w_out_b, w1_b, w2_b, l, lc, tm, final=l == depth - 1)
    return xs
```
